```python
import math
import jax, jax.numpy as jnp
from jax import lax
import numpy as np

D_MODEL = 4096
BATCH = 1
SEQ = 8192
DEPTH = 4

N_MIXERS = 2
N_HEADS = 32
HEAD_DIM = 128
D_FF = ((8 * D_MODEL // 3 + 255) // 256) * 256
NUM_BUCKETS = 32
MAX_DISTANCE = 128
Q_BLOCK = 128

NSA_KV_GROUPS = 4
NSA_GROUP_HEADS = N_HEADS // NSA_KV_GROUPS
CMP_LEN = 32
CMP_STRIDE = 16
SEL_BLOCK = 64
SEL_COUNT = 16
WINDOW = 512
FORCE_SCORE = 1e4
NSA_IN = N_HEADS * HEAD_DIM + 6 * NSA_KV_GROUPS * HEAD_DIM + 3 * N_HEADS

Q_RANK = 1024
KV_RANK = 512
IDX_HEADS = 32
IDX_DIM = 128
IDX_TOPK = 256
DSA_IN = Q_RANK + KV_RANK + IDX_DIM + IDX_HEADS

N_NSA = (DEPTH + 1) // 2
N_DSA = DEPTH // 2

kernel_name = "hybrid_nsa_dsa_interleaved_trunk"


def rmsnorm(x, g, eps=1e-6):
    xf = x.astype(jnp.float32)
    y = xf * lax.rsqrt(jnp.mean(xf * xf, axis=-1, keepdims=True) + eps)
    return (y * g.astype(jnp.float32)).astype(x.dtype)


def layernorm(x, g, b, eps=1e-6):
    xf = x.astype(jnp.float32)
    mu = jnp.mean(xf, axis=-1, keepdims=True)
    xc = xf - mu
    y = xc * lax.rsqrt(jnp.mean(xc * xc, axis=-1, keepdims=True) + eps)
    return (y * g.astype(jnp.float32) + b.astype(jnp.float32)).astype(x.dtype)


def t5_bucket(dist):
    n = jnp.maximum(dist, 0)
    max_exact = NUM_BUCKETS // 2
    nf = jnp.maximum(n, 1).astype(jnp.float32)
    large = max_exact + (jnp.log(nf / max_exact) / math.log(MAX_DISTANCE / max_exact)
                         * (NUM_BUCKETS - max_exact)).astype(jnp.int32)
    large = jnp.minimum(large, NUM_BUCKETS - 1)
    return jnp.where(n < max_exact, n, large)


def masked_softmax(logits, mask):
    lf = jnp.where(mask, logits.astype(jnp.float32), -1e30)
    m = jnp.max(lf, axis=-1, keepdims=True)
    p = jnp.where(mask, jnp.exp(lf - m), 0.0)
    return p / jnp.maximum(jnp.sum(p, axis=-1, keepdims=True), 1e-30)


def swiglu(h, w_gate, w_up, w_down):
    return (jax.nn.silu(h @ w_gate) * (h @ w_up)) @ w_down


def nsa_mixer(h, rel_bias, w_in, pos_k, pos_v, w1_k, w2_k, w1_v, w2_v, w_out):
    B, S, _ = h.shape
    G, R, D, H = NSA_KV_GROUPS, NSA_GROUP_HEADS, HEAD_DIM, N_HEADS
    proj = h @ w_in
    sizes = [H * D] + [G * D] * 6 + [3 * H]
    q, kc, vc, ks, vs, kw, vw, gl = jnp.split(proj, np.cumsum(sizes)[:-1].tolist(), axis=-1)
    q = q.reshape(B, S, G, R, D)
    kc, vc, ks, vs, kw, vw = [a.reshape(B, S, G, D) for a in (kc, vc, ks, vs, kw, vw)]
    gates = jax.nn.sigmoid(gl.astype(jnp.float32)).reshape(B, S, 3, G, R)

    n_cmp = (S - CMP_LEN) // CMP_STRIDE + 1
    cmp_idx_np = np.arange(n_cmp)[:, None] * CMP_STRIDE + np.arange(CMP_LEN)[None, :]
    cmp_idx = jnp.asarray(cmp_idx_np, dtype=jnp.int32)

    def compress(k, pos, w1, w2):
        kb = k[:, cmp_idx] + pos[None, None, :, None, :]
        kb = jnp.moveaxis(kb, 3, 2).reshape(B, n_cmp, G, CMP_LEN * D)
        return jax.nn.silu(kb @ w1) @ w2

    k_cmp = compress(kc, pos_k, w1_k, w2_k)
    v_cmp = compress(vc, pos_v, w1_v, w2_v)
    cmp_end = jnp.asarray(cmp_idx_np[:, -1], dtype=jnp.int32)

    n_sel = S // SEL_BLOCK
    sel_start_np = np.arange(n_sel) * SEL_BLOCK
    overlap = jnp.asarray(((cmp_idx_np[:, 0][:, None] < sel_start_np[None, :] + SEL_BLOCK)
                           & (cmp_idx_np[:, -1][:, None] >= sel_start_np[None, :])).astype(np.float32))
    sel_start = jnp.asarray(sel_start_np, dtype=jnp.int32)
    n_top = min(SEL_COUNT, n_sel)

    kw_pad = jnp.pad(kw, ((0, 0), (WINDOW, 0), (0, 0), (0, 0)))
    vw_pad = jnp.pad(vw, ((0, 0), (WINDOW, 0), (0, 0), (0, 0)))
    kwin = Q_BLOCK + WINDOW
    scale = HEAD_DIM ** -0.5
    bias_t = rel_bias.astype(jnp.float32).T
    bias_gr = bias_t.reshape(G, R, NUM_BUCKETS)
    bi = jnp.arange(B)[:, None, None, None]
    gi = jnp.arange(G)[None, :, None, None]
    g_ix = jnp.arange(G)[None, :, None, None, None]
    r_ix = jnp.arange(R)[None, None, :, None, None]

    def block(qi):
        q0 = qi * Q_BLOCK
        t = q0 + jnp.arange(Q_BLOCK, dtype=jnp.int32)
        qb = lax.dynamic_slice_in_dim(q, q0, Q_BLOCK, axis=1)
        gb = lax.dynamic_slice_in_dim(gates, q0, Q_BLOCK, axis=1)

        dist_c = t[:, None] - cmp_end[None, :]
        bias_c = bias_t[:, t5_bucket(dist_c)].reshape(G, R, Q_BLOCK, n_cmp)
        s_c = jnp.einsum('bqgrd,bcgd->bgrqc', qb, k_cmp).astype(jnp.float32) * scale + bias_c
        p_c = masked_softmax(s_c, dist_c >= 0)
        o_c = jnp.einsum('bgrqc,bcgd->bqgrd', p_c.astype(v_cmp.dtype), v_cmp)

        imp = jnp.einsum('bgrqc,cj->bgqj', p_c, overlap)
        blk_t = t // SEL_BLOCK
        j = jnp.arange(n_sel, dtype=jnp.int32)
        forced = (j[None, :] == 0) | (j[None, :] == blk_t[:, None]) | (j[None, :] == blk_t[:, None] - 1)
        imp = jnp.where(forced, FORCE_SCORE, imp)
        imp = jnp.where(sel_start[None, :] <= t[:, None], imp, -1.0)
        _, top = lax.top_k(imp, n_top)
        tok = (top[..., None] * SEL_BLOCK + jnp.arange(SEL_BLOCK, dtype=jnp.int32)
               ).reshape(B, G, Q_BLOCK, n_top * SEL_BLOCK)
        k_sel = ks[bi, tok, gi]
        v_sel = vs[bi, tok, gi]
        dist_s = t[None, None, :, None] - tok
        bias_s = bias_gr[g_ix, r_ix, t5_bucket(dist_s)[:, :, None]]
        s_s = jnp.einsum('bqgrd,bgqnd->bgrqn', qb, k_sel).astype(jnp.float32) * scale + bias_s
        p_s = masked_softmax(s_s, (dist_s >= 0)[:, :, None])
        o_s = jnp.einsum('bgrqn,bgqnd->bqgrd', p_s.astype(v_sel.dtype), v_sel)

        kwb = lax.dynamic_slice_in_dim(kw_pad, q0, kwin, axis=1)
        vwb = lax.dynamic_slice_in_dim(vw_pad, q0, kwin, axis=1)
        s_pos = q0 - WINDOW + jnp.arange(kwin, dtype=jnp.int32)
        dist_w = t[:, None] - s_pos[None, :]
        mask_w = (dist_w >= 0) & (dist_w < WINDOW) & (s_pos[None, :] >= 0)
        bias_w = bias_t[:, t5_bucket(dist_w)].reshape(G, R, Q_BLOCK, kwin)
        s_w = jnp.einsum('bqgrd,bkgd->bgrqk', qb, kwb).astype(jnp.float32) * scale + bias_w
        p_w = masked_softmax(s_w, mask_w)
        o_w = jnp.einsum('bgrqk,bkgd->bqgrd', p_w.astype(vwb.dtype), vwb)

        o = (gb[:, :, 0, :, :, None] * o_c + gb[:, :, 1, :, :, None] * o_s
             + gb[:, :, 2, :, :, None] * o_w)
        return o.astype(h.dtype)

    out = lax.map(block, jnp.arange(S // Q_BLOCK))
    out = jnp.moveaxis(out, 0, 1).reshape(B, S, H * D)
    return out @ w_out


def dsa_mixer(h, rel_bias, w_in, g_q, g_kv, w_uq, w_uk, w_uv, w_qidx, ln_g, ln_b, w_out):
    B, S, _ = h.shape
    H, D = N_HEADS, HEAD_DIM
    proj = h @ w_in
    sizes = [Q_RANK, KV_RANK, IDX_DIM, IDX_HEADS]
    c_q, c_kv, k_idx, w_idx = jnp.split(proj, np.cumsum(sizes)[:-1].tolist(), axis=-1)
    c_q = rmsnorm(c_q, g_q)
    c_kv = rmsnorm(c_kv, g_kv)
    q = (c_q @ w_uq).reshape(B, S, H, D)
    q_idx = (c_q @ w_qidx).reshape(B, S, IDX_HEADS, IDX_DIM)
    k_idx = layernorm(k_idx, ln_g, ln_b)
    w_idx = w_idx.astype(jnp.float32) * IDX_HEADS ** -0.5
    top_k = min(IDX_TOPK, S // 4)
    scale = HEAD_DIM ** -0.5
    idx_scale = IDX_DIM ** -0.5
    s_all = jnp.arange(S, dtype=jnp.int32)
    bi = jnp.arange(B)[:, None, None]

    def block(qi):
        q0 = qi * Q_BLOCK
        t = q0 + jnp.arange(Q_BLOCK, dtype=jnp.int32)
        qib = lax.dynamic_slice_in_dim(q_idx, q0, Q_BLOCK, axis=1)
        wb = lax.dynamic_slice_in_dim(w_idx, q0, Q_BLOCK, axis=1)
        sc = jnp.einsum('bqhd,bsd->bqhs', qib, k_idx).astype(jnp.float32) * idx_scale
        score = jnp.einsum('bqhs,bqh->bqs', jax.nn.relu(sc), wb)
        score = jnp.where(s_all[None, None, :] <= t[None, :, None], score, -jnp.inf)
        _, sel = lax.top_k(score, top_k)
        c_sel = c_kv[bi, sel]
        qb = lax.dynamic_slice_in_dim(q, q0, Q_BLOCK, axis=1)
        q_lat = jnp.einsum('bqhd,hdc->bqhc', qb, w_uk)
        dist = t[None, :, None] - sel
        bias = jnp.moveaxis(rel_bias.astype(jnp.float32)[t5_bucket(dist)], -1, 1)
        logits = jnp.einsum('bqhc,bqkc->bhqk', q_lat, c_sel).astype(jnp.float32) * scale + bias
        p = masked_softmax(logits, (dist >= 0)[:, None])
        o_lat = jnp.einsum('bhqk,bqkc->bqhc', p.astype(c_sel.dtype), c_sel)
        return jnp.einsum('bqhc,hcd->bqhd', o_lat, w_uv)

    out = lax.map(block, jnp.arange(S // Q_BLOCK))
    out = jnp.moveaxis(out, 0, 1).reshape(B, S, H * D)
    return out @ w_out


def setup_inputs(seed: int = 0) -> dict:
    key = jax.random.key(seed)
    ks = jax.random.split(key, 28)
    f32 = jnp.float32

    def nrm(k, shape, fan_in):
        return jax.random.normal(k, shape, f32) * (fan_in ** -0.5)

    def gain(k, shape):
        return 1.0 + 0.01 * jax.random.normal(k, shape, f32)

    D, F, HD = D_MODEL, D_FF, HEAD_DIM
    return {
        "x": jax.random.normal(ks[0], (BATCH, SEQ, D), f32),
        "rel_bias": 0.5 * jax.random.normal(ks[1], (NUM_BUCKETS, N_HEADS), f32),
        "norm_mix": gain(ks[2], (DEPTH, D)),
        "norm_ffn": gain(ks[3], (DEPTH, D)),
        "norm_final": gain(ks[4], (D,)),
        "ffn_w_gate": nrm(ks[5], (DEPTH, D, F), D),
        "ffn_w_up": nrm(ks[6], (DEPTH, D, F), D),
        "ffn_w_down": nrm(ks[7], (DEPTH, F, D), F),
        "nsa_w_in": nrm(ks[8], (N_NSA, D, NSA_IN), D),
        "nsa_cmp_pos_k": 0.02 * jax.random.normal(ks[9], (N_NSA, CMP_LEN, HD), f32),
        "nsa_cmp_pos_v": 0.02 * jax.random.normal(ks[10], (N_NSA, CMP_LEN, HD), f32),
        "nsa_cmp_w1_k": nrm(ks[11], (N_NSA, CMP_LEN * HD, HD), CMP_LEN * HD),
        "nsa_cmp_w2_k": nrm(ks[12], (N_NSA, HD, HD), HD),
        "nsa_cmp_w1_v": nrm(ks[13], (N_NSA, CMP_LEN * HD, HD), CMP_LEN * HD),
        "nsa_cmp_w2_v": nrm(ks[14], (N_NSA, HD, HD), HD),
        "nsa_w_out": nrm(ks[15], (N_NSA, N_HEADS * HD, D), N_HEADS * HD),
        "dsa_w_in": nrm(ks[16], (N_DSA, D, DSA_IN), D),
        "dsa_norm_q": gain(ks[17], (N_DSA, Q_RANK)),
        "dsa_norm_kv": gain(ks[18], (N_DSA, KV_RANK)),
        "dsa_w_uq": nrm(ks[19], (N_DSA, Q_RANK, N_HEADS * HD), Q_RANK),
        "dsa_w_uk": nrm(ks[20], (N_DSA, N_HEADS, HD, KV_RANK), KV_RANK),
        "dsa_w_uv": nrm(ks[21], (N_DSA, N_HEADS, KV_RANK, HD), KV_RANK),
        "dsa_w_qidx": nrm(ks[22], (N_DSA, Q_RANK, IDX_HEADS * IDX_DIM), Q_RANK),
        "dsa_idx_ln_g": gain(ks[23], (N_DSA, IDX_DIM)),
        "dsa_idx_ln_b": 0.01 * jax.random.normal(ks[24], (N_DSA, IDX_DIM), f32),
        "dsa_w_out": nrm(ks[25], (N_DSA, N_HEADS * HD, D), N_HEADS * HD),
    }


def reference(x, rel_bias, norm_mix, norm_ffn, norm_final, ffn_w_gate, ffn_w_up, ffn_w_down,
              nsa_w_in, nsa_cmp_pos_k, nsa_cmp_pos_v, nsa_cmp_w1_k, nsa_cmp_w2_k,
              nsa_cmp_w1_v, nsa_cmp_w2_v, nsa_w_out,
              dsa_w_in, dsa_norm_q, dsa_norm_kv, dsa_w_uq, dsa_w_uk, dsa_w_uv,
              dsa_w_qidx, dsa_idx_ln_g, dsa_idx_ln_b, dsa_w_out):
    h = x
    for i in range(DEPTH):
        hn = rmsnorm(h, norm_mix[i])
        a = i // N_MIXERS
        if i % N_MIXERS == 0:
            mix = nsa_mixer(hn, rel_bias, nsa_w_in[a], nsa_cmp_pos_k[a], nsa_cmp_pos_v[a],
                            nsa_cmp_w1_k[a], nsa_cmp_w2_k[a], nsa_cmp_w1_v[a], nsa_cmp_w2_v[a],
                            nsa_w_out[a])
        else:
            mix = dsa_mixer(hn, rel_bias, dsa_w_in[a], dsa_norm_q[a], dsa_norm_kv[a], dsa_w_uq[a],
                            dsa_w_uk[a], dsa_w_uv[a], dsa_w_qidx[a], dsa_idx_ln_g[a],
                            dsa_idx_ln_b[a], dsa_w_out[a])
        h = h + mix.astype(h.dtype)
        hn = rmsnorm(h, norm_ffn[i])
        h = h + swiglu(hn, ffn_w_gate[i], ffn_w_up[i], ffn_w_down[i]).astype(h.dtype)
    return rmsnorm(h, norm_final)
```

```python
import functools
import math

import numpy as np
import jax
import jax.numpy as jnp
from jax import lax
from jax.experimental import pallas as pl
from jax.experimental.pallas import tpu as pltpu

N_HEADS = 32
HEAD_DIM = 128
NUM_BUCKETS = 32
MAX_DISTANCE = 128
Q_BLOCK = 128
NSA_KV_GROUPS = 4
NSA_GROUP_HEADS = N_HEADS // NSA_KV_GROUPS
CMP_LEN = 32
CMP_STRIDE = 16
SEL_BLOCK = 64
SEL_COUNT = 16
WINDOW = 512
FORCE_SCORE = 1e4
Q_RANK = 1024
KV_RANK = 512
IDX_HEADS = 32
IDX_DIM = 128
IDX_TOPK = 256
RMS_EPS = 1e-6

LANES = 128
V7X_VMEM_LIMIT_BYTES = 56 * 1024 * 1024
KEY_TILE = 512
CHUNK = 128

F32 = jnp.float32
BF16 = jnp.bfloat16
NEG_INF = float("-inf")
M_INIT = -1e30
LOG2E = math.log2(math.e)
INT_MIN = -(2 ** 31)


def _cparams(sem):
    return pltpu.CompilerParams(dimension_semantics=sem,
                                vmem_limit_bytes=V7X_VMEM_LIMIT_BYTES)


def _pick(n, candidates):
    for c in candidates:
        if n % c == 0:
            return c
    return n


def _round_up(n, m):
    return (n + m - 1) // m * m


def _rmsnorm_kernel(x_ref, g_ref, o_ref):
    x = x_ref[...].astype(F32)
    ms = jnp.mean(x * x, axis=-1, keepdims=True)
    o_ref[...] = (x * lax.rsqrt(ms + RMS_EPS) * g_ref[...]).astype(o_ref.dtype)


def _layernorm_kernel(x_ref, g_ref, b_ref, o_ref):
    x = x_ref[...].astype(F32)
    mu = jnp.mean(x, axis=-1, keepdims=True)
    xc = x - mu
    var = jnp.mean(xc * xc, axis=-1, keepdims=True)
    o_ref[...] = (xc * lax.rsqrt(var + RMS_EPS) * g_ref[...] + b_ref[...]).astype(o_ref.dtype)


def _rownorm(x, gain, out_dtype, *, width=None, col_block=0, bias=None):
    m = x.shape[0]
    width = x.shape[1] if width is None else width
    tm = _pick(m, (256, 128, 64, 32, 16, 8))
    row = lambda i: (i, col_block)
    vec = lambda i: (0, 0)
    g2 = gain.reshape(1, width).astype(F32)
    if bias is None:
        kern, extra, extra_specs = _rmsnorm_kernel, (), ()
    else:
        kern, extra = _layernorm_kernel, (bias.reshape(1, width).astype(F32),)
        extra_specs = (pl.BlockSpec((1, width), vec),)
    return pl.pallas_call(
        kern,
        grid=(m // tm,),
        in_specs=[pl.BlockSpec((tm, width), row), pl.BlockSpec((1, width), vec), *extra_specs],
        out_specs=pl.BlockSpec((tm, width), lambda i: (i, 0)),
        out_shape=jax.ShapeDtypeStruct((m, width), out_dtype),
        compiler_params=_cparams(("parallel",)),
    )(x, g2, *extra)


def _mm_kernel(*refs, nk, has_res):
    if has_res:
        a_ref, b_ref, r_ref, o_ref, acc_ref = refs
    else:
        a_ref, b_ref, o_ref, acc_ref = refs
    k = pl.program_id(2)

    @pl.when(k == 0)
    def _():
        acc_ref[...] = jnp.zeros_like(acc_ref)

    acc_ref[...] += jnp.dot(a_ref[...], b_ref[...], preferred_element_type=F32)

    @pl.when(k == nk - 1)
    def _():
        acc = acc_ref[...]
        if has_res:
            acc = acc + r_ref[...]
        o_ref[...] = acc.astype(o_ref.dtype)


def _matmul(a, b, out_dtype, res=None):
    m, kd = a.shape
    n = b.shape[1]
    tm = _pick(m, (1024, 512, 256, 128))
    tn = _pick(n, (1024, 896, 768, 512, 384, 256, 128))
    tk = _pick(kd, (1024, 512, 256, 128))
    nk = kd // tk
    in_specs = [pl.BlockSpec((tm, tk), lambda i, j, k: (i, k)),
                pl.BlockSpec((tk, tn), lambda i, j, k: (k, j))]
    args = [a, b]
    if res is not None:
        in_specs.append(pl.BlockSpec((tm, tn), lambda i, j, k: (i, j)))
        args.append(res)
    return pl.pallas_call(
        functools.partial(_mm_kernel, nk=nk, has_res=res is not None),
        grid=(m // tm, n // tn, nk),
        in_specs=in_specs,
        out_specs=pl.BlockSpec((tm, tn), lambda i, j, k: (i, j)),
        out_shape=jax.ShapeDtypeStruct((m, n), out_dtype),
        scratch_shapes=[pltpu.VMEM((tm, tn), F32)],
        compiler_params=_cparams(("parallel", "parallel", "arbitrary")),
    )(*args)


def _swiglu_kernel(x_ref, wg_ref, wu_ref, o_ref, accg_ref, accu_ref, *, nk):
    k = pl.program_id(2)

    @pl.when(k == 0)
    def _():
        accg_ref[...] = jnp.zeros_like(accg_ref)
        accu_ref[...] = jnp.zeros_like(accu_ref)

    x = x_ref[...]
    accg_ref[...] += jnp.dot(x, wg_ref[...], preferred_element_type=F32)
    accu_ref[...] += jnp.dot(x, wu_ref[...], preferred_element_type=F32)

    @pl.when(k == nk - 1)
    def _():
        g = accg_ref[...]
        o_ref[...] = (g * jax.nn.sigmoid(g) * accu_ref[...]).astype(o_ref.dtype)


def _swiglu_up(x, wg, wu):
    m, kd = x.shape
    n = wg.shape[1]
    tm = _pick(m, (1024, 512, 256, 128))
    tn = _pick(n, (512, 256, 128))
    tk = _pick(kd, (1024, 512, 256, 128))
    nk = kd // tk
    return pl.pallas_call(
        functools.partial(_swiglu_kernel, nk=nk),
        grid=(m // tm, n // tn, nk),
        in_specs=[pl.BlockSpec((tm, tk), lambda i, j, k: (i, k)),
                  pl.BlockSpec((tk, tn), lambda i, j, k: (k, j)),
                  pl.BlockSpec((tk, tn), lambda i, j, k: (k, j))],
        out_specs=pl.BlockSpec((tm, tn), lambda i, j, k: (i, j)),
        out_shape=jax.ShapeDtypeStruct((m, n), BF16),
        scratch_shapes=[pltpu.VMEM((tm, tn), F32), pltpu.VMEM((tm, tn), F32)],
        compiler_params=_cparams(("parallel", "parallel", "arbitrary")),
    )(x, wg, wu)


def _dot_nt(a, b):
    return lax.dot_general(a, b, (((1,), (1,)), ((), ())), preferred_element_type=F32)


def _softmax_step(q, k, v, add, m_ref, l_ref, acc_ref, h):
    z = _dot_nt(q, k) + add
    m_old = m_ref[h]
    m_new = jnp.maximum(m_old, jnp.max(z, axis=1, keepdims=True))
    alpha = jnp.exp2(m_old - m_new)
    p = jnp.exp2(z - m_new)
    l_ref[h] = alpha * l_ref[h] + jnp.sum(p, axis=1, keepdims=True)
    acc_ref[h] = alpha * acc_ref[h] + jnp.dot(p.astype(BF16), v, preferred_element_type=F32)
    m_ref[h] = m_new


def _softmax_init(m_ref, l_ref, acc_ref):
    m_ref[...] = jnp.full_like(m_ref, M_INIT)
    l_ref[...] = jnp.zeros_like(l_ref)
    acc_ref[...] = jnp.zeros_like(acc_ref)


def _softmax_finish(o_ref, l_ref, acc_ref, nh):
    for h in range(nh):
        inv = 1.0 / jnp.maximum(l_ref[h], 1e-30)
        o_ref[:, h * HEAD_DIM:(h + 1) * HEAD_DIM] = (acc_ref[h] * inv).astype(o_ref.dtype)


def _t5_bucket(dist):
    n = jnp.maximum(dist, 0)
    max_exact = NUM_BUCKETS // 2
    nf = jnp.maximum(n, 1).astype(F32)
    large = max_exact + (jnp.log(nf / max_exact) / math.log(MAX_DISTANCE / max_exact)
                         * (NUM_BUCKETS - max_exact)).astype(jnp.int32)
    large = jnp.minimum(large, NUM_BUCKETS - 1)
    return jnp.where(n < max_exact, n, large)


def _bias_tiles(rel_bias):
    tab = (rel_bias.astype(F32).T - rel_bias.astype(F32)[NUM_BUCKETS - 1][:, None]) * LOG2E
    i = np.arange(Q_BLOCK)[:, None]
    j = np.arange(2 * CHUNK)[None, :]
    dist_near = jnp.asarray(np.where(j < CHUNK, CHUNK + i - j, i - (j - CHUNK)), jnp.int32)
    near = tab[:, _t5_bucket(dist_near)]
    lane = np.arange(LANES)[None, :]
    dist_cmp = jnp.asarray(i - CMP_STRIDE * (lane - LANES // 2) - (CMP_LEN - 1), jnp.int32)
    cmp_ = jnp.where(dist_cmp >= 0, tab[:, _t5_bucket(dist_cmp)], 0.0)
    return near, cmp_


def _compress_kernel(x_ref, pos_ref, w1_ref, w2_ref, o_ref, *, half):
    x = x_ref[...].astype(F32)
    xa = (x + pos_ref[0:1, :]).astype(BF16)
    xb = (x + pos_ref[1:2, :]).astype(BF16)
    a = jnp.dot(xa, w1_ref[0:half, :], preferred_element_type=F32)
    b = jnp.dot(xb, w1_ref[half:2 * half, :], preferred_element_type=F32)
    nc = a.shape[0]
    pre = a + pltpu.roll(b, nc - 1, 0)
    hmid = pre * jax.nn.sigmoid(pre)
    o_ref[...] = jnp.dot(hmid.astype(BF16), w2_ref[...], preferred_element_type=F32).astype(o_ref.dtype)


def _compress(kv_rows, pos, w1, w2):
    two, g, nc, wd = kv_rows.shape
    d = w2.shape[-1]
    return pl.pallas_call(
        functools.partial(_compress_kernel, half=wd),
        grid=(two, g),
        in_specs=[pl.BlockSpec((None, None, nc, wd), lambda a, b: (a, b, 0, 0)),
                  pl.BlockSpec((None, 2, wd), lambda a, b: (a, 0, 0)),
                  pl.BlockSpec((None, 2 * wd, d), lambda a, b: (a, 0, 0)),
                  pl.BlockSpec((None, d, d), lambda a, b: (a, 0, 0))],
        out_specs=pl.BlockSpec((None, None, nc, d), lambda a, b: (a, b, 0, 0)),
        out_shape=jax.ShapeDtypeStruct((two, g, nc, d), BF16),
        compiler_params=_cparams(("parallel", "parallel")),
    )(kv_rows, pos, w1, w2)


def _nsa_cmp_kernel(q_ref, kc_ref, vc_ref, ov_ref, pat_ref, o_ref, sel_ref, *, n_top):
    qi = pl.program_id(1)
    q0 = qi * Q_BLOCK
    ncp = kc_ref.shape[0]
    nsel = ov_ref.shape[1]
    rows = lax.broadcasted_iota(jnp.int32, (Q_BLOCK, ncp), 0)
    cols = lax.broadcasted_iota(jnp.int32, (Q_BLOCK, ncp), 1)
    add_mask = jnp.where(cols * CMP_STRIDE + (CMP_LEN - 1) <= q0 + rows, 0.0, NEG_INF)
    shift = (qi * (Q_BLOCK // CMP_STRIDE) + (ncp - LANES // 2)) % ncp
    kc = kc_ref[...]
    vc = vc_ref[...]
    ov = ov_ref[...]
    imp = jnp.zeros((Q_BLOCK, nsel), F32)
    for r in range(NSA_GROUP_HEADS):
        pat = pat_ref[r]
        if ncp > LANES:
            pat = jnp.concatenate([pat, jnp.zeros((Q_BLOCK, ncp - LANES), F32)], axis=1)
        bias = pltpu.roll(pat, shift, 1)
        z = _dot_nt(q_ref[:, r * HEAD_DIM:(r + 1) * HEAD_DIM], kc) + bias + add_mask
        m = jnp.maximum(jnp.max(z, axis=1, keepdims=True), M_INIT)
        e = jnp.exp2(z - m)
        p = (e * (1.0 / jnp.maximum(jnp.sum(e, axis=1, keepdims=True), 1e-30))).astype(BF16)
        o_ref[:, r * HEAD_DIM:(r + 1) * HEAD_DIM] = jnp.dot(
            p, vc, preferred_element_type=F32).astype(o_ref.dtype)
        imp = imp + jnp.dot(p, ov, preferred_element_type=F32)

    t = q0 + lax.broadcasted_iota(jnp.int32, (Q_BLOCK, nsel), 0)
    j = lax.broadcasted_iota(jnp.int32, (Q_BLOCK, nsel), 1)
    blk_t = t // SEL_BLOCK
    forced = (j == 0) | (j == blk_t) | (j == blk_t - 1)
    imp = jnp.where(forced, FORCE_SCORE, imp)
    imp = jnp.where(j * SEL_BLOCK <= t, imp, -1.0)
    jf = j.astype(F32)

    def pick(_, carry):
        imp_c, sel_c = carry
        best = jnp.max(imp_c, axis=1, keepdims=True)
        first = jnp.min(jnp.where(imp_c == best, jf, float(nsel)), axis=1, keepdims=True)
        hit = jf == first
        return jnp.where(hit, NEG_INF, imp_c), jnp.where(hit, 1.0, sel_c)

    _, sel = lax.fori_loop(0, n_top, pick, (imp, jnp.zeros((Q_BLOCK, nsel), F32)))
    sel_ref[...] = sel.astype(sel_ref.dtype)


def _nsa_compressed(q, kv_cmp, overlap, pat_cmp):
    s = q.shape[0]
    g, ncp, d = kv_cmp.shape[1:]
    nsel = overlap.shape[1]
    gw = NSA_GROUP_HEADS * HEAD_DIM
    return pl.pallas_call(
        functools.partial(_nsa_cmp_kernel, n_top=min(SEL_COUNT, nsel)),
        grid=(g, s // Q_BLOCK),
        in_specs=[pl.BlockSpec((Q_BLOCK, gw), lambda a, b: (b, a)),
                  pl.BlockSpec((None, None, ncp, d), lambda a, b: (0, a, 0, 0)),
                  pl.BlockSpec((None, None, ncp, d), lambda a, b: (1, a, 0, 0)),
                  pl.BlockSpec((ncp, nsel), lambda a, b: (0, 0)),
                  pl.BlockSpec((NSA_GROUP_HEADS, Q_BLOCK, LANES), lambda a, b: (a, 0, 0))],
        out_specs=[pl.BlockSpec((Q_BLOCK, gw), lambda a, b: (b, a)),
                   pl.BlockSpec((None, Q_BLOCK, nsel), lambda a, b: (a, b, 0))],
        out_shape=[jax.ShapeDtypeStruct((s, N_HEADS * HEAD_DIM), BF16),
                   jax.ShapeDtypeStruct((g, s, nsel), BF16)],
        compiler_params=_cparams(("parallel", "parallel")),
    )(q, kv_cmp, kv_cmp, overlap, pat_cmp)


def _masked_attn_kernel(*refs, nh, kv_shared, mode):
    if mode == "sel":
        q_ref, k_ref, v_ref, bias_ref, sel_ref, et_ref, o_ref, m_ref, l_ref, acc_ref = refs
    else:
        q_ref, k_ref, v_ref, bias_ref, mask_ref, o_ref, m_ref, l_ref, acc_ref = refs
    qi = pl.program_id(1)
    q0 = qi * Q_BLOCK
    far_end = q0 - CHUNK
    _softmax_init(m_ref, l_ref, acc_ref)

    def kv_cols(h):
        c = 0 if kv_shared else h
        return slice(c * HEAD_DIM, (c + 1) * HEAD_DIM)

    def mask_rows(start, width):
        if mode == "sel":
            hit = _dot_nt(sel_ref[...], et_ref[pl.ds(start, width), :])
            return jnp.where(hit > 0.5, 0.0, NEG_INF)
        c0 = start // CHUNK
        parts = [mask_ref[c0 + c].astype(F32) for c in range(width // CHUNK)]
        return parts[0] if len(parts) == 1 else jnp.concatenate(parts, axis=1)

    def far_tile(kt, carry):
        start = pl.multiple_of(kt * KEY_TILE, KEY_TILE)
        cols = start + lax.broadcasted_iota(jnp.int32, (Q_BLOCK, KEY_TILE), 1)
        add = jnp.where(cols < far_end, mask_rows(start, KEY_TILE), NEG_INF)
        for h in range(nh):
            _softmax_step(q_ref[:, h * HEAD_DIM:(h + 1) * HEAD_DIM],
                          k_ref[pl.ds(start, KEY_TILE), kv_cols(h)],
                          v_ref[pl.ds(start, KEY_TILE), kv_cols(h)],
                          add, m_ref, l_ref, acc_ref, h)
        return carry

    n_far = (jnp.maximum(far_end, 0) + KEY_TILE - 1) // KEY_TILE
    lax.fori_loop(0, n_far, far_tile, 0)

    prev0 = pl.multiple_of(jnp.maximum(far_end, 0), CHUNK)
    diag0 = pl.multiple_of(q0, CHUNK)
    rows = lax.broadcasted_iota(jnp.int32, (Q_BLOCK, CHUNK), 0)
    cols = lax.broadcasted_iota(jnp.int32, (Q_BLOCK, CHUNK), 1)
    add_prev = jnp.where(qi > 0, mask_rows(prev0, CHUNK), NEG_INF)
    add_diag = jnp.where(cols <= rows, mask_rows(diag0, CHUNK), NEG_INF)
    add_near = jnp.concatenate([add_prev, add_diag], axis=1)
    for h in range(nh):
        k = jnp.concatenate([k_ref[pl.ds(prev0, CHUNK), kv_cols(h)],
                             k_ref[pl.ds(diag0, CHUNK), kv_cols(h)]], axis=0)
        v = jnp.concatenate([v_ref[pl.ds(prev0, CHUNK), kv_cols(h)],
                             v_ref[pl.ds(diag0, CHUNK), kv_cols(h)]], axis=0)
        _softmax_step(q_ref[:, h * HEAD_DIM:(h + 1) * HEAD_DIM], k, v,
                      add_near + bias_ref[h], m_ref, l_ref, acc_ref, h)
    _softmax_finish(o_ref, l_ref, acc_ref, nh)


def _masked_attention(q, k, v, bias_near, *, mode, nh, kv_shared, sel=None, expand_t=None, mask=None):
    s = q.shape[0]
    n_groups = N_HEADS // nh
    qw = nh * HEAD_DIM
    kw = HEAD_DIM if kv_shared else qw
    in_specs = [pl.BlockSpec((Q_BLOCK, qw), lambda a, b: (b, a)),
                pl.BlockSpec((s, kw), lambda a, b: (0, a)),
                pl.BlockSpec((s, kw), lambda a, b: (0, a)),
                pl.BlockSpec((nh, Q_BLOCK, 2 * CHUNK), lambda a, b: (a, 0, 0))]
    args = [q, k, v, bias_near]
    if mode == "sel":
        nsel = sel.shape[-1]
        in_specs += [pl.BlockSpec((None, Q_BLOCK, nsel), lambda a, b: (a, b, 0)),
                     pl.BlockSpec((s, nsel), lambda a, b: (0, 0))]
        args += [sel, expand_t]
    else:
        nch = s // CHUNK
        in_specs += [pl.BlockSpec((None, nch, Q_BLOCK, CHUNK), lambda a, b: (b, 0, 0, 0))]
        args += [mask]
    return pl.pallas_call(
        functools.partial(_masked_attn_kernel, nh=nh, kv_shared=kv_shared, mode=mode),
        grid=(n_groups, s // Q_BLOCK),
        in_specs=in_specs,
        out_specs=pl.BlockSpec((Q_BLOCK, qw), lambda a, b: (b, a)),
        out_shape=jax.ShapeDtypeStruct((s, N_HEADS * HEAD_DIM), BF16),
        scratch_shapes=[pltpu.VMEM((nh, Q_BLOCK, 1), F32),
                        pltpu.VMEM((nh, Q_BLOCK, 1), F32),
                        pltpu.VMEM((nh, Q_BLOCK, HEAD_DIM), F32)],
        compiler_params=_cparams(("parallel", "arbitrary")),
    )(*args)


def _nsa_win_kernel(q_ref, k_ref, v_ref, bias_ref, gl_ref, oc_ref, os_ref, o_ref,
                    m_ref, l_ref, acc_ref, *, n_far):
    g = pl.program_id(0)
    qi = pl.program_id(1)
    q0 = qi * Q_BLOCK
    _softmax_init(m_ref, l_ref, acc_ref)
    rows = lax.broadcasted_iota(jnp.int32, (Q_BLOCK, CHUNK), 0)
    cols = lax.broadcasted_iota(jnp.int32, (Q_BLOCK, CHUNK), 1)
    adds, ks, vs = [], [], []
    for c in range(n_far + 2):
        back = n_far + 1 - c
        start = pl.multiple_of(jnp.maximum(q0 - back * CHUNK, 0), CHUNK)
        valid = qi >= back
        if c == 0:
            ok = valid & (cols > rows)
        elif back == 0:
            ok = cols <= rows
        else:
            ok = jnp.broadcast_to(valid, (Q_BLOCK, CHUNK))
        adds.append(jnp.where(ok, 0.0, NEG_INF))
        ks.append(k_ref[pl.ds(start, CHUNK), :])
        vs.append(v_ref[pl.ds(start, CHUNK), :])
    k_far, v_far = jnp.concatenate(ks[:n_far], axis=0), jnp.concatenate(vs[:n_far], axis=0)
    k_near, v_near = jnp.concatenate(ks[n_far:], axis=0), jnp.concatenate(vs[n_far:], axis=0)
    add_far = jnp.concatenate(adds[:n_far], axis=1)
    add_near = jnp.concatenate(adds[n_far:], axis=1)
    gates = jax.nn.sigmoid(gl_ref[...])
    for h in range(NSA_GROUP_HEADS):
        qh = q_ref[:, h * HEAD_DIM:(h + 1) * HEAD_DIM]
        _softmax_step(qh, k_far, v_far, add_far, m_ref, l_ref, acc_ref, h)
        _softmax_step(qh, k_near, v_near, add_near + bias_ref[h], m_ref, l_ref, acc_ref, h)
    lane = lax.broadcasted_iota(jnp.int32, (Q_BLOCK, LANES), 1)
    for h in range(NSA_GROUP_HEADS):
        sl = slice(h * HEAD_DIM, (h + 1) * HEAD_DIM)
        head = g * NSA_GROUP_HEADS + h
        gate = [jnp.sum(jnp.where(lane == br * N_HEADS + head, gates, 0.0), axis=1, keepdims=True)
                for br in range(3)]
        o_w = acc_ref[h] * (1.0 / jnp.maximum(l_ref[h], 1e-30))
        o = (gate[0] * oc_ref[:, sl].astype(F32) + gate[1] * os_ref[:, sl].astype(F32)
             + gate[2] * o_w)
        o_ref[:, sl] = o.astype(o_ref.dtype)


def _nsa_window_combine(q, kw, vw, bias_near, gate_logits, o_c, o_s):
    s = q.shape[0]
    g = NSA_KV_GROUPS
    gw = NSA_GROUP_HEADS * HEAD_DIM
    n_far = WINDOW // CHUNK - 1
    blk = pl.BlockSpec((Q_BLOCK, gw), lambda a, b: (b, a))
    return pl.pallas_call(
        functools.partial(_nsa_win_kernel, n_far=n_far),
        grid=(g, s // Q_BLOCK),
        in_specs=[blk,
                  pl.BlockSpec((s, HEAD_DIM), lambda a, b: (0, a)),
                  pl.BlockSpec((s, HEAD_DIM), lambda a, b: (0, a)),
                  pl.BlockSpec((NSA_GROUP_HEADS, Q_BLOCK, 2 * CHUNK), lambda a, b: (a, 0, 0)),
                  pl.BlockSpec((Q_BLOCK, LANES), lambda a, b: (b, 0)),
                  blk, blk],
        out_specs=blk,
        out_shape=jax.ShapeDtypeStruct((s, N_HEADS * HEAD_DIM), BF16),
        scratch_shapes=[pltpu.VMEM((NSA_GROUP_HEADS, Q_BLOCK, 1), F32),
                        pltpu.VMEM((NSA_GROUP_HEADS, Q_BLOCK, 1), F32),
                        pltpu.VMEM((NSA_GROUP_HEADS, Q_BLOCK, HEAD_DIM), F32)],
        compiler_params=_cparams(("parallel", "parallel")),
    )(q, kw, vw, bias_near, gate_logits, o_c, o_s)


def _dsa_index_kernel(qx_ref, w_ref, kx_ref, mask_ref, key_ref, *, top_k, w_scale):
    qi = pl.program_id(0)
    q0 = qi * Q_BLOCK
    n_ch = qi + 1
    nch_total = key_ref.shape[0]
    w = w_ref[...] * w_scale
    lane = lax.broadcasted_iota(jnp.int32, (Q_BLOCK, LANES), 1)
    w_cols = [jnp.sum(jnp.where(lane == h, w, 0.0), axis=1, keepdims=True) for h in range(IDX_HEADS)]
    rows = lax.broadcasted_iota(jnp.int32, (Q_BLOCK, CHUNK), 0)
    cols = lax.broadcasted_iota(jnp.int32, (Q_BLOCK, CHUNK), 1)

    def score_chunk(ch, carry):
        start = pl.multiple_of(ch * CHUNK, CHUNK)
        kx = kx_ref[pl.ds(start, CHUNK), :]
        acc = jnp.zeros((Q_BLOCK, CHUNK), F32)
        for h in range(IDX_HEADS):
            sc = _dot_nt(qx_ref[:, h * IDX_DIM:(h + 1) * IDX_DIM], kx)
            acc = acc + jnp.maximum(sc, 0.0) * w_cols[h]
        acc = jnp.where(start + cols <= q0 + rows, acc, NEG_INF)
        bits = lax.bitcast_convert_type(acc, jnp.int32)
        key_ref[ch] = jnp.where(bits >= 0, bits, bits ^ jnp.int32(0x7FFFFFFF))
        return carry

    lax.fori_loop(0, n_ch, score_chunk, 0)

    def count_ge(cand):
        def body(ch, cnt):
            return cnt + jnp.where(key_ref[ch] >= cand, 1.0, 0.0)
        cnt = lax.fori_loop(0, n_ch, body, jnp.zeros((Q_BLOCK, CHUNK), F32))
        return jnp.sum(cnt, axis=1, keepdims=True)

    def bisect(b, thr):
        cand = thr + lax.shift_left(jnp.int32(1), jnp.int32(31) - b)
        return jnp.where(count_ge(cand) >= float(top_k), cand, thr)

    thr = lax.fori_loop(0, 32, bisect, jnp.full((Q_BLOCK, 1), INT_MIN, jnp.int32))

    def write_chunk(ch, carry):
        keep = (key_ref[ch] >= thr) & (ch * CHUNK + cols <= q0 + rows)
        mask_ref[ch] = jnp.where(keep, 0.0, NEG_INF).astype(mask_ref.dtype)
        return carry

    lax.fori_loop(0, n_ch, write_chunk, 0)

    def blank_chunk(ch, carry):
        mask_ref[ch] = jnp.full((Q_BLOCK, CHUNK), NEG_INF, mask_ref.dtype)
        return carry

    lax.fori_loop(n_ch, nch_total, blank_chunk, 0)


def _dsa_index_mask(q_idx, w_idx_src, w_col_block, k_idx, top_k):
    s = q_idx.shape[0]
    nq, nch = s // Q_BLOCK, s // CHUNK
    w_scale = IDX_HEADS ** -0.5 * IDX_DIM ** -0.5
    return pl.pallas_call(
        functools.partial(_dsa_index_kernel, top_k=top_k, w_scale=w_scale),
        grid=(nq,),
        in_specs=[pl.BlockSpec((Q_BLOCK, IDX_HEADS * IDX_DIM), lambda i: (i, 0)),
                  pl.BlockSpec((Q_BLOCK, LANES), lambda i: (i, w_col_block)),
                  pl.BlockSpec((s, IDX_DIM), lambda i: (0, 0))],
        out_specs=pl.BlockSpec((None, nch, Q_BLOCK, CHUNK), lambda i: (i, 0, 0, 0)),
        out_shape=jax.ShapeDtypeStruct((nq, nch, Q_BLOCK, CHUNK), BF16),
        scratch_shapes=[pltpu.VMEM((nch, Q_BLOCK, CHUNK), jnp.int32)],
        compiler_params=_cparams(("parallel",)),
    )(q_idx, w_idx_src, k_idx)


def _nsa_constants(s):
    ncp = s // CMP_STRIDE
    nsel = s // SEL_BLOCK
    c0 = np.arange(ncp) * CMP_STRIDE
    sel0 = np.arange(nsel) * SEL_BLOCK
    overlap = ((c0[:, None] < sel0[None, :] + SEL_BLOCK)
               & (c0[:, None] + CMP_LEN - 1 >= sel0[None, :])).astype(np.float32)
    expand_t = (np.arange(s)[:, None] // SEL_BLOCK == np.arange(nsel)[None, :]).astype(np.float32)
    return jnp.asarray(overlap, BF16), jnp.asarray(expand_t, BF16)


def _nsa_mixer(hn, h_res, bias_near, pat_cmp, w_in, pos_k, pos_v, w1_k, w2_k, w1_v, w2_v, w_out):
    s = hn.shape[0]
    g, d = NSA_KV_GROUPS, HEAD_DIM
    hd = N_HEADS * d
    q_scale = d ** -0.5 * LOG2E
    w_q = (w_in[:, :hd] * q_scale).astype(BF16)
    w_kv = w_in[:, hd:hd + 6 * g * d].astype(BF16)
    w_g = jnp.pad(w_in[:, hd + 6 * g * d:], ((0, 0), (0, LANES - 3 * N_HEADS))).astype(BF16)
    q = _matmul(hn, w_q, BF16)
    kv = _matmul(hn, w_kv, BF16)
    gate_logits = _matmul(hn, w_g, F32)
    gd = g * d
    kc, vc, ks, vs, kw, vw = [kv[:, i * gd:(i + 1) * gd] for i in range(6)]

    def rows16(a):
        return a.reshape(s, g, d).transpose(1, 0, 2).reshape(g, s // CMP_STRIDE, CMP_STRIDE * d)

    pos = jnp.stack([pos_k.reshape(2, CMP_STRIDE * d), pos_v.reshape(2, CMP_STRIDE * d)]).astype(F32)
    kv_cmp = _compress(jnp.stack([rows16(kc), rows16(vc)]), pos,
                       jnp.stack([w1_k, w1_v]).astype(BF16), jnp.stack([w2_k, w2_v]).astype(BF16))
    overlap, expand_t = _nsa_constants(s)
    o_c, sel = _nsa_compressed(q, kv_cmp, overlap, pat_cmp)
    o_s = _masked_attention(q, ks, vs, bias_near, mode="sel", nh=NSA_GROUP_HEADS, kv_shared=True,
                            sel=sel, expand_t=expand_t)
    o = _nsa_window_combine(q, kw, vw, bias_near, gate_logits, o_c, o_s)
    return _matmul(o, w_out.astype(BF16), F32, res=h_res)


def _dsa_mixer(hn, h_res, bias_near, w_in, g_q, g_kv, w_uq, w_uk, w_uv, w_qidx, ln_g, ln_b, w_out):
    s = hn.shape[0]
    d = HEAD_DIM
    n_in = w_in.shape[1]
    w_in_p = jnp.pad(w_in, ((0, 0), (0, _round_up(n_in, LANES) - n_in))).astype(BF16)
    proj = _matmul(hn, w_in_p, F32)
    c_q = _rownorm(proj, g_q, BF16, width=Q_RANK, col_block=0)
    c_kv = _rownorm(proj, g_kv, BF16, width=KV_RANK, col_block=Q_RANK // KV_RANK)
    k_idx = _rownorm(proj, ln_g, BF16, width=IDX_DIM, col_block=(Q_RANK + KV_RANK) // IDX_DIM, bias=ln_b)
    q_scale = d ** -0.5 * LOG2E
    q = _matmul(c_q, (w_uq * q_scale).astype(BF16), BF16)
    q_idx = _matmul(c_q, w_qidx.astype(BF16), BF16)
    w_k = w_uk.transpose(2, 0, 1).reshape(KV_RANK, N_HEADS * d).astype(BF16)
    w_v = w_uv.transpose(1, 0, 2).reshape(KV_RANK, N_HEADS * d).astype(BF16)
    k = _matmul(c_kv, w_k, BF16)
    v = _matmul(c_kv, w_v, BF16)
    top_k = min(IDX_TOPK, s // 4)
    mask = _dsa_index_mask(q_idx, proj, (Q_RANK + KV_RANK + IDX_DIM) // LANES, k_idx, top_k)
    o = _masked_attention(q, k, v, bias_near, mode="dsa", nh=4, kv_shared=False, mask=mask)
    return _matmul(o, w_out.astype(BF16), F32, res=h_res)


def _ffn(hn, h_res, w_gate, w_up, w_down):
    f = w_gate.shape[1]
    fp = _round_up(f, 1024) if f > 1024 else _round_up(f, LANES)
    wg = jnp.pad(w_gate.astype(BF16), ((0, 0), (0, fp - f)))
    wu = jnp.pad(w_up.astype(BF16), ((0, 0), (0, fp - f)))
    wd = jnp.pad(w_down.astype(BF16), ((0, fp - f), (0, 0)))
    return _matmul(_swiglu_up(hn, wg, wu), wd, F32, res=h_res)


def kernel(x, rel_bias, norm_mix, norm_ffn, norm_final, ffn_w_gate, ffn_w_up, ffn_w_down, nsa_w_in, nsa_cmp_pos_k, nsa_cmp_pos_v, nsa_cmp_w1_k, nsa_cmp_w2_k, nsa_cmp_w1_v, nsa_cmp_w2_v, nsa_w_out, dsa_w_in, dsa_norm_q, dsa_norm_kv, dsa_w_uq, dsa_w_uk, dsa_w_uv, dsa_w_qidx, dsa_idx_ln_g, dsa_idx_ln_b, dsa_w_out):
    b, s, dm = x.shape
    assert s % KEY_TILE == 0 and s // CMP_STRIDE >= LANES
    depth = norm_mix.shape[0]
    bias_near, pat_cmp = _bias_tiles(rel_bias)
    outs = []
    for bi in range(b):
        h = x[bi]
        for i in range(depth):
            hn = _rownorm(h, norm_mix[i], BF16)
            a = i // 2
            if i % 2 == 0:
                h = _nsa_mixer(hn, h, bias_near, pat_cmp, nsa_w_in[a], nsa_cmp_pos_k[a], nsa_cmp_pos_v[a],
                               nsa_cmp_w1_k[a], nsa_cmp_w2_k[a], nsa_cmp_w1_v[a], nsa_cmp_w2_v[a],
                               nsa_w_out[a])
            else:
                h = _dsa_mixer(hn, h, bias_near, dsa_w_in[a], dsa_norm_q[a], dsa_norm_kv[a], dsa_w_uq[a],
                               dsa_w_uk[a], dsa_w_uv[a], dsa_w_qidx[a], dsa_idx_ln_g[a],
                               dsa_idx_ln_b[a], dsa_w_out[a])
            hn = _rownorm(h, norm_ffn[i], BF16)
            h = _ffn(hn, h, ffn_w_gate[i], ffn_w_up[i], ffn_w_down[i])
        outs.append(_rownorm(h, norm_final, F32))
    return jnp.stack(outs)
```

```python
import functools
import math

import numpy as np
import jax
import jax.numpy as jnp
from jax import lax
from jax.experimental import pallas as pl
from jax.experimental.pallas import tpu as pltpu

N_HEADS = 32
HEAD_DIM = 128
NUM_BUCKETS = 32
MAX_DISTANCE = 128
Q_BLOCK = 128
NSA_KV_GROUPS = 4
NSA_GROUP_HEADS = N_HEADS // NSA_KV_GROUPS
CMP_LEN = 32
CMP_STRIDE = 16
SEL_BLOCK = 64
SEL_COUNT = 16
WINDOW = 512
FORCE_SCORE = 1e4
Q_RANK = 1024
KV_RANK = 512
IDX_HEADS = 32
IDX_DIM = 128
IDX_TOPK = 256
RMS_EPS = 1e-6

LANES = 128
V7X_VMEM_LIMIT_BYTES = 56 * 1024 * 1024
KEY_TILE = 512
CHUNK = 128

F32 = jnp.float32
BF16 = jnp.bfloat16
NEG_INF = float("-inf")
M_INIT = -1e30
LOG2E = math.log2(math.e)
INT_MIN = -(2 ** 31)


def _cparams(sem):
    return pltpu.CompilerParams(dimension_semantics=sem,
                                vmem_limit_bytes=V7X_VMEM_LIMIT_BYTES)


def _pick(n, candidates):
    for c in candidates:
        if n % c == 0:
            return c
    return n


def _round_up(n, m):
    return (n + m - 1) // m * m


def _rmsnorm_kernel(x_ref, g_ref, o_ref):
    x = x_ref[...].astype(F32)
    ms = jnp.mean(x * x, axis=-1, keepdims=True)
    o_ref[...] = (x * lax.rsqrt(ms + RMS_EPS) * g_ref[...]).astype(o_ref.dtype)


def _layernorm_kernel(x_ref, g_ref, b_ref, o_ref):
    x = x_ref[...].astype(F32)
    mu = jnp.mean(x, axis=-1, keepdims=True)
    xc = x - mu
    var = jnp.mean(xc * xc, axis=-1, keepdims=True)
    o_ref[...] = (xc * lax.rsqrt(var + RMS_EPS) * g_ref[...] + b_ref[...]).astype(o_ref.dtype)


def _rownorm(x, gain, out_dtype, *, width=None, col_block=0, bias=None):
    m = x.shape[0]
    width = x.shape[1] if width is None else width
    tm = _pick(m, (256, 128, 64, 32, 16, 8))
    row = lambda i: (i, col_block)
    vec = lambda i: (0, 0)
    g2 = gain.reshape(1, width).astype(F32)
    if bias is None:
        kern, extra, extra_specs = _rmsnorm_kernel, (), ()
    else:
        kern, extra = _layernorm_kernel, (bias.reshape(1, width).astype(F32),)
        extra_specs = (pl.BlockSpec((1, width), vec),)
    return pl.pallas_call(
        kern,
        grid=(m // tm,),
        in_specs=[pl.BlockSpec((tm, width), row), pl.BlockSpec((1, width), vec), *extra_specs],
        out_specs=pl.BlockSpec((tm, width), lambda i: (i, 0)),
        out_shape=jax.ShapeDtypeStruct((m, width), out_dtype),
        name="rownorm",
        compiler_params=_cparams(("parallel",)),
    )(x, g2, *extra)


def _mm_kernel(*refs, nk, has_res):
    a_ref, b_ref = refs[:2]
    r_ref = refs[2] if has_res else None
    o_ref = refs[3] if has_res else refs[2]
    acc_ref = refs[-1] if nk > 1 else None
    k = pl.program_id(2)
    part = jnp.dot(a_ref[...], b_ref[...], preferred_element_type=F32)

    def finish(acc):
        if has_res:
            acc = acc + r_ref[...]
        o_ref[...] = acc.astype(o_ref.dtype)

    if nk == 1:
        finish(part)
        return

    @pl.when(k == 0)
    def _():
        acc_ref[...] = part

    @pl.when((k > 0) & (k < nk - 1))
    def _():
        acc_ref[...] += part

    @pl.when(k == nk - 1)
    def _():
        finish(acc_ref[...] + part)


MAX_FULL_K = 4096


def _matmul(a, b, out_dtype, res=None):
    m, kd = a.shape
    n = b.shape[1]
    tm = _pick(m, (1024, 512, 256, 128))
    if kd <= MAX_FULL_K:
        tk = kd
        tn_cap = min(2048, max(512, (2 * 1024 * 1024) // kd))
        tn = _pick(n, [c for c in (2048, 1024, 896, 512, 384, 256, 128) if c <= max(tn_cap, 896)])
    else:
        tk = _pick(kd, (2816, 2048, 1024, 512, 256, 128))
        tn = _pick(n, (1024, 512, 256, 128))
    nk = kd // tk
    in_specs = [pl.BlockSpec((tm, tk), lambda i, j, k: (i, k)),
                pl.BlockSpec((tk, tn), lambda i, j, k: (k, j))]
    args = [a, b]
    if res is not None:
        in_specs.append(pl.BlockSpec((tm, tn), lambda i, j, k: (i, j)))
        args.append(res)
    return pl.pallas_call(
        functools.partial(_mm_kernel, nk=nk, has_res=res is not None),
        grid=(m // tm, n // tn, nk),
        in_specs=in_specs,
        out_specs=pl.BlockSpec((tm, tn), lambda i, j, k: (i, j)),
        out_shape=jax.ShapeDtypeStruct((m, n), out_dtype),
        scratch_shapes=[pltpu.VMEM((tm, tn), F32)] if nk > 1 else [],
        name="matmul",
        compiler_params=_cparams(("parallel", "parallel", "arbitrary")),
    )(*args)


def _swiglu_kernel(x_ref, wg_ref, wu_ref, o_ref):
    x = x_ref[...]
    g = jnp.dot(x, wg_ref[...], preferred_element_type=F32)
    u = jnp.dot(x, wu_ref[...], preferred_element_type=F32)
    o_ref[...] = (g * jax.nn.sigmoid(g) * u).astype(o_ref.dtype)


def _swiglu_up(x, wg, wu):
    m, kd = x.shape
    n = wg.shape[1]
    assert kd <= MAX_FULL_K
    tm = _pick(m, (1024, 512, 256, 128))
    tn = _pick(n, (512, 256, 128))
    return pl.pallas_call(
        _swiglu_kernel,
        grid=(m // tm, n // tn),
        in_specs=[pl.BlockSpec((tm, kd), lambda i, j: (i, 0)),
                  pl.BlockSpec((kd, tn), lambda i, j: (0, j)),
                  pl.BlockSpec((kd, tn), lambda i, j: (0, j))],
        out_specs=pl.BlockSpec((tm, tn), lambda i, j: (i, j)),
        out_shape=jax.ShapeDtypeStruct((m, n), BF16),
        name="swiglu_up",
        compiler_params=_cparams(("parallel", "parallel")),
    )(x, wg, wu)


def _dot_nt(a, b):
    return lax.dot_general(a, b, (((1,), (1,)), ((), ())), preferred_element_type=F32)


def _with_ones(v):
    return jnp.concatenate([v, jnp.ones_like(v)], axis=1)


def _lane_blocks(x):
    return [x[:, c * LANES:(c + 1) * LANES] for c in range(x.shape[1] // LANES)]


def _softmax_step(q, k, v_ones, add, m_ref, acc_ref, h):
    z = _dot_nt(q, k) + add
    m_old = m_ref[h]
    m_new = jnp.maximum(m_old, jnp.max(z, axis=1, keepdims=True))
    alpha = jnp.exp2(m_old - m_new)
    p = jnp.concatenate([jnp.exp2(zc - m_new) for zc in _lane_blocks(z)], axis=1).astype(BF16)
    pv = jnp.dot(p, v_ones, preferred_element_type=F32)
    acc_ref[h] = jnp.concatenate([alpha, alpha], axis=1) * acc_ref[h] + pv
    m_ref[h] = m_new


def _softmax_init(m_ref, acc_ref):
    m_ref[...] = jnp.full_like(m_ref, M_INIT)
    acc_ref[...] = jnp.zeros_like(acc_ref)


def _normalized(acc):
    return acc[:, :HEAD_DIM] * (1.0 / jnp.maximum(acc[:, HEAD_DIM:], 1e-30))


def _attend_once(q, k, v_ones, add):
    z = _dot_nt(q, k) + add
    m = jnp.maximum(jnp.max(z, axis=1, keepdims=True), M_INIT)
    p = jnp.exp2(z - m).astype(BF16)
    return _normalized(jnp.dot(p, v_ones, preferred_element_type=F32))


def _t5_bucket(dist):
    n = jnp.maximum(dist, 0)
    max_exact = NUM_BUCKETS // 2
    nf = jnp.maximum(n, 1).astype(F32)
    large = max_exact + (jnp.log(nf / max_exact) / math.log(MAX_DISTANCE / max_exact)
                         * (NUM_BUCKETS - max_exact)).astype(jnp.int32)
    large = jnp.minimum(large, NUM_BUCKETS - 1)
    return jnp.where(n < max_exact, n, large)


def _bias_tiles(rel_bias):
    tab = (rel_bias.astype(F32).T - rel_bias.astype(F32)[NUM_BUCKETS - 1][:, None]) * LOG2E
    i = np.arange(Q_BLOCK)[:, None]
    j = np.arange(2 * CHUNK)[None, :]
    dist_near = jnp.asarray(np.where(j < CHUNK, CHUNK + i - j, i - (j - CHUNK)), jnp.int32)
    near = tab[:, _t5_bucket(dist_near)]
    lane = np.arange(LANES)[None, :]
    dist_cmp = jnp.asarray(i - CMP_STRIDE * (lane - LANES // 2) - (CMP_LEN - 1), jnp.int32)
    cmp_ = jnp.where(dist_cmp >= 0, tab[:, _t5_bucket(dist_cmp)], 0.0)
    return near, cmp_


def _compress_kernel(x_ref, pos_ref, w1_ref, w2_ref, o_ref, *, half):
    x = x_ref[...].astype(F32)
    xa = (x + pos_ref[0:1, :]).astype(BF16)
    xb = (x + pos_ref[1:2, :]).astype(BF16)
    a = jnp.dot(xa, w1_ref[0:half, :], preferred_element_type=F32)
    b = jnp.dot(xb, w1_ref[half:2 * half, :], preferred_element_type=F32)
    nc = a.shape[0]
    pre = a + pltpu.roll(b, nc - 1, 0)
    hmid = pre * jax.nn.sigmoid(pre)
    o_ref[...] = jnp.dot(hmid.astype(BF16), w2_ref[...], preferred_element_type=F32).astype(o_ref.dtype)


def _compress(kv_rows, pos, w1, w2):
    two, g, nc, wd = kv_rows.shape
    d = w2.shape[-1]
    return pl.pallas_call(
        functools.partial(_compress_kernel, half=wd),
        grid=(two, g),
        in_specs=[pl.BlockSpec((None, None, nc, wd), lambda a, b: (a, b, 0, 0)),
                  pl.BlockSpec((None, 2, wd), lambda a, b: (a, 0, 0)),
                  pl.BlockSpec((None, 2 * wd, d), lambda a, b: (a, 0, 0)),
                  pl.BlockSpec((None, d, d), lambda a, b: (a, 0, 0))],
        out_specs=pl.BlockSpec((None, None, nc, d), lambda a, b: (a, b, 0, 0)),
        out_shape=jax.ShapeDtypeStruct((two, g, nc, d), BF16),
        name="nsa_compress",
        compiler_params=_cparams(("parallel", "parallel")),
    )(kv_rows, pos, w1, w2)


def _nsa_cmp_kernel(q_ref, kc_ref, vc_ref, ov_ref, pat_ref, o_ref, sel_ref, *, n_top):
    qi = pl.program_id(1)
    q0 = qi * Q_BLOCK
    ncp = kc_ref.shape[0]
    nsel = ov_ref.shape[1]
    rows = lax.broadcasted_iota(jnp.int32, (Q_BLOCK, ncp), 0)
    cols = lax.broadcasted_iota(jnp.int32, (Q_BLOCK, ncp), 1)
    add_mask = jnp.where(cols * CMP_STRIDE + (CMP_LEN - 1) <= q0 + rows, 0.0, NEG_INF)
    shift = (qi * (Q_BLOCK // CMP_STRIDE) + (ncp - LANES // 2)) % ncp
    kc = kc_ref[...]
    rhs = jnp.concatenate([vc_ref[...], ov_ref[...], jnp.ones((ncp, LANES), BF16)], axis=1)
    imp = jnp.zeros((Q_BLOCK, nsel), F32)
    for r in range(NSA_GROUP_HEADS):
        pat = pat_ref[r]
        if ncp > LANES:
            pat = jnp.concatenate([pat, jnp.zeros((Q_BLOCK, ncp - LANES), F32)], axis=1)
        bias = pltpu.roll(pat, shift, 1)
        z = _dot_nt(q_ref[:, r * HEAD_DIM:(r + 1) * HEAD_DIM], kc) + bias + add_mask
        m = jnp.maximum(jnp.max(z, axis=1, keepdims=True), M_INIT)
        e = jnp.exp2(z - m).astype(BF16)
        acc = jnp.dot(e, rhs, preferred_element_type=F32)
        inv = 1.0 / jnp.maximum(acc[:, HEAD_DIM + nsel:], 1e-30)
        o_ref[:, r * HEAD_DIM:(r + 1) * HEAD_DIM] = (acc[:, :HEAD_DIM] * inv).astype(o_ref.dtype)
        imp = imp + acc[:, HEAD_DIM:HEAD_DIM + nsel] * inv[:, :nsel]

    t = q0 + lax.broadcasted_iota(jnp.int32, (Q_BLOCK, nsel), 0)
    j = lax.broadcasted_iota(jnp.int32, (Q_BLOCK, nsel), 1)
    blk_t = t // SEL_BLOCK
    forced = (j == 0) | (j == blk_t) | (j == blk_t - 1)
    imp = jnp.where(forced, FORCE_SCORE, imp)
    imp = jnp.where(j * SEL_BLOCK <= t, imp, -1.0)
    jf = j.astype(F32)

    def pick(_, carry):
        imp_c, sel_c = carry
        best = jnp.max(imp_c, axis=1, keepdims=True)
        first = jnp.min(jnp.where(imp_c == best, jf, float(nsel)), axis=1, keepdims=True)
        hit = jf == first
        return jnp.where(hit, NEG_INF, imp_c), jnp.where(hit, 1.0, sel_c)

    _, sel = lax.fori_loop(0, n_top, pick, (imp, jnp.zeros((Q_BLOCK, nsel), F32)))
    sel_ref[...] = sel.astype(sel_ref.dtype)


def _nsa_compressed(q, kv_cmp, overlap, pat_cmp):
    s = q.shape[0]
    g, ncp, d = kv_cmp.shape[1:]
    nsel = overlap.shape[1]
    gw = NSA_GROUP_HEADS * HEAD_DIM
    return pl.pallas_call(
        functools.partial(_nsa_cmp_kernel, n_top=min(SEL_COUNT, nsel)),
        grid=(g, s // Q_BLOCK),
        in_specs=[pl.BlockSpec((Q_BLOCK, gw), lambda a, b: (b, a)),
                  pl.BlockSpec((None, None, ncp, d), lambda a, b: (0, a, 0, 0)),
                  pl.BlockSpec((None, None, ncp, d), lambda a, b: (1, a, 0, 0)),
                  pl.BlockSpec((ncp, nsel), lambda a, b: (0, 0)),
                  pl.BlockSpec((NSA_GROUP_HEADS, Q_BLOCK, LANES), lambda a, b: (a, 0, 0))],
        out_specs=[pl.BlockSpec((Q_BLOCK, gw), lambda a, b: (b, a)),
                   pl.BlockSpec((None, Q_BLOCK, nsel), lambda a, b: (a, b, 0))],
        out_shape=[jax.ShapeDtypeStruct((s, N_HEADS * HEAD_DIM), BF16),
                   jax.ShapeDtypeStruct((g, s, nsel), BF16)],
        name="nsa_cmp_attn",
        compiler_params=_cparams(("parallel", "parallel")),
    )(q, kv_cmp, kv_cmp, overlap, pat_cmp)


def _masked_attn_kernel(*refs, nh, kv_shared, mode):
    if mode == "sel":
        q_ref, k_ref, v_ref, bias_ref, sel_ref, et_ref, o_ref, m_ref, acc_ref = refs
    else:
        q_ref, k_ref, v_ref, bias_ref, mask_ref, o_ref, m_ref, acc_ref = refs
    qi = pl.program_id(1)
    q0 = qi * Q_BLOCK
    far_end = q0 - CHUNK
    _softmax_init(m_ref, acc_ref)

    def kv_cols(h):
        c = 0 if kv_shared else h
        return slice(c * HEAD_DIM, (c + 1) * HEAD_DIM)

    def mask_rows(start, width):
        if mode == "sel":
            hit = _dot_nt(sel_ref[...], et_ref[pl.ds(start, width), :])
            return jnp.where(hit > 0.5, 0.0, NEG_INF)
        c0 = start // CHUNK
        parts = [mask_ref[c0 + c].astype(F32) for c in range(width // CHUNK)]
        return parts[0] if len(parts) == 1 else jnp.concatenate(parts, axis=1)

    def far_tile(kt, carry):
        start = pl.multiple_of(kt * KEY_TILE, KEY_TILE)
        cols = start + lax.broadcasted_iota(jnp.int32, (Q_BLOCK, KEY_TILE), 1)
        add = jnp.where(cols < far_end, mask_rows(start, KEY_TILE), NEG_INF)
        for h in range(nh):
            _softmax_step(q_ref[:, h * HEAD_DIM:(h + 1) * HEAD_DIM],
                          k_ref[pl.ds(start, KEY_TILE), kv_cols(h)],
                          _with_ones(v_ref[pl.ds(start, KEY_TILE), kv_cols(h)]),
                          add, m_ref, acc_ref, h)
        return carry

    n_far = (jnp.maximum(far_end, 0) + KEY_TILE - 1) // KEY_TILE
    lax.fori_loop(0, n_far, far_tile, 0)

    prev0 = pl.multiple_of(jnp.maximum(far_end, 0), CHUNK)
    diag0 = pl.multiple_of(q0, CHUNK)
    rows = lax.broadcasted_iota(jnp.int32, (Q_BLOCK, CHUNK), 0)
    cols = lax.broadcasted_iota(jnp.int32, (Q_BLOCK, CHUNK), 1)
    add_prev = jnp.where(qi > 0, mask_rows(prev0, CHUNK), NEG_INF)
    add_diag = jnp.where(cols <= rows, mask_rows(diag0, CHUNK), NEG_INF)
    add_near = jnp.concatenate([add_prev, add_diag], axis=1)
    for h in range(nh):
        k = jnp.concatenate([k_ref[pl.ds(prev0, CHUNK), kv_cols(h)],
                             k_ref[pl.ds(diag0, CHUNK), kv_cols(h)]], axis=0)
        v = jnp.concatenate([v_ref[pl.ds(prev0, CHUNK), kv_cols(h)],
                             v_ref[pl.ds(diag0, CHUNK), kv_cols(h)]], axis=0)
        _softmax_step(q_ref[:, h * HEAD_DIM:(h + 1) * HEAD_DIM], k, _with_ones(v),
                      add_near + bias_ref[h], m_ref, acc_ref, h)
    for h in range(nh):
        o_ref[:, h * HEAD_DIM:(h + 1) * HEAD_DIM] = _normalized(acc_ref[h]).astype(o_ref.dtype)


def _masked_attention(q, k, v, bias_near, *, mode, nh, kv_shared, sel=None, expand_t=None, mask=None):
    s = q.shape[0]
    n_groups = N_HEADS // nh
    qw = nh * HEAD_DIM
    kw = HEAD_DIM if kv_shared else qw
    in_specs = [pl.BlockSpec((Q_BLOCK, qw), lambda a, b: (b, a)),
                pl.BlockSpec((s, kw), lambda a, b: (0, a)),
                pl.BlockSpec((s, kw), lambda a, b: (0, a)),
                pl.BlockSpec((nh, Q_BLOCK, 2 * CHUNK), lambda a, b: (a, 0, 0))]
    args = [q, k, v, bias_near]
    if mode == "sel":
        nsel = sel.shape[-1]
        in_specs += [pl.BlockSpec((None, Q_BLOCK, nsel), lambda a, b: (a, b, 0)),
                     pl.BlockSpec((s, nsel), lambda a, b: (0, 0))]
        args += [sel, expand_t]
    else:
        nch = s // CHUNK
        in_specs += [pl.BlockSpec((None, nch, Q_BLOCK, CHUNK), lambda a, b: (b, 0, 0, 0))]
        args += [mask]
    return pl.pallas_call(
        functools.partial(_masked_attn_kernel, nh=nh, kv_shared=kv_shared, mode=mode),
        grid=(n_groups, s // Q_BLOCK),
        in_specs=in_specs,
        out_specs=pl.BlockSpec((Q_BLOCK, qw), lambda a, b: (b, a)),
        out_shape=jax.ShapeDtypeStruct((s, N_HEADS * HEAD_DIM), BF16),
        scratch_shapes=[pltpu.VMEM((nh, Q_BLOCK, LANES), F32),
                        pltpu.VMEM((nh, Q_BLOCK, 2 * HEAD_DIM), F32)],
        name="masked_attn_" + mode,
        compiler_params=_cparams(("parallel", "arbitrary")),
    )(*args)


def _nsa_win_kernel(q_ref, k_ref, v_ref, bias_ref, gl_ref, oc_ref, os_ref, o_ref, *, n_far):
    g = pl.program_id(0)
    qi = pl.program_id(1)
    q0 = qi * Q_BLOCK
    rows = lax.broadcasted_iota(jnp.int32, (Q_BLOCK, CHUNK), 0)
    cols = lax.broadcasted_iota(jnp.int32, (Q_BLOCK, CHUNK), 1)
    adds, ks, vs = [], [], []
    for c in range(n_far + 2):
        back = n_far + 1 - c
        start = pl.multiple_of(jnp.maximum(q0 - back * CHUNK, 0), CHUNK)
        valid = qi >= back
        if c == 0:
            ok = valid & (cols > rows)
        elif back == 0:
            ok = cols <= rows
        else:
            ok = jnp.broadcast_to(valid, (Q_BLOCK, CHUNK))
        adds.append(jnp.where(ok, 0.0, NEG_INF))
        ks.append(k_ref[pl.ds(start, CHUNK), :])
        vs.append(v_ref[pl.ds(start, CHUNK), :])
    k_all = jnp.concatenate(ks, axis=0)
    v_all = _with_ones(jnp.concatenate(vs, axis=0))
    add_far = jnp.concatenate(adds[:n_far], axis=1)
    add_near = jnp.concatenate(adds[n_far:], axis=1)
    gates = jax.nn.sigmoid(gl_ref[...])
    lane = lax.broadcasted_iota(jnp.int32, (Q_BLOCK, LANES), 1)
    for h in range(NSA_GROUP_HEADS):
        sl = slice(h * HEAD_DIM, (h + 1) * HEAD_DIM)
        head = g * NSA_GROUP_HEADS + h
        gate = [jnp.sum(jnp.where(lane == br * N_HEADS + head, gates, 0.0), axis=1, keepdims=True)
                for br in range(3)]
        add = jnp.concatenate([add_far, add_near + bias_ref[h]], axis=1)
        o_w = _attend_once(q_ref[:, sl], k_all, v_all, add)
        o = (gate[0] * oc_ref[:, sl].astype(F32) + gate[1] * os_ref[:, sl].astype(F32)
             + gate[2] * o_w)
        o_ref[:, sl] = o.astype(o_ref.dtype)


def _nsa_window_combine(q, kw, vw, bias_near, gate_logits, o_c, o_s):
    s = q.shape[0]
    g = NSA_KV_GROUPS
    gw = NSA_GROUP_HEADS * HEAD_DIM
    n_far = WINDOW // CHUNK - 1
    blk = pl.BlockSpec((Q_BLOCK, gw), lambda a, b: (b, a))
    return pl.pallas_call(
        functools.partial(_nsa_win_kernel, n_far=n_far),
        grid=(g, s // Q_BLOCK),
        in_specs=[blk,
                  pl.BlockSpec((s, HEAD_DIM), lambda a, b: (0, a)),
                  pl.BlockSpec((s, HEAD_DIM), lambda a, b: (0, a)),
                  pl.BlockSpec((NSA_GROUP_HEADS, Q_BLOCK, 2 * CHUNK), lambda a, b: (a, 0, 0)),
                  pl.BlockSpec((Q_BLOCK, LANES), lambda a, b: (b, 0)),
                  blk, blk],
        out_specs=blk,
        out_shape=jax.ShapeDtypeStruct((s, N_HEADS * HEAD_DIM), BF16),
        name="nsa_window",
        compiler_params=_cparams(("parallel", "parallel")),
    )(q, kw, vw, bias_near, gate_logits, o_c, o_s)


def _dsa_index_kernel(qx_ref, w_ref, kx_ref, mask_ref, key_ref, wb_ref, jp_ref, *, top_k, w_scale):
    qi = pl.program_id(0)
    q0 = qi * Q_BLOCK
    n_ch = qi + 1
    nch_total = key_ref.shape[0]
    s_total = nch_total * CHUNK
    kf = float(top_k)
    w = w_ref[...] * w_scale
    lane = lax.broadcasted_iota(jnp.int32, (Q_BLOCK, LANES), 1)
    for h in range(IDX_HEADS):
        col = jnp.sum(jnp.where(lane == h, w, 0.0), axis=1, keepdims=True)
        wb_ref[h] = jnp.broadcast_to(col, (Q_BLOCK, LANES))
    rows = lax.broadcasted_iota(jnp.int32, (Q_BLOCK, CHUNK), 0)
    cols = lax.broadcasted_iota(jnp.int32, (Q_BLOCK, CHUNK), 1)
    rows2 = lax.broadcasted_iota(jnp.int32, (Q_BLOCK, 2 * CHUNK), 0)
    cols2 = lax.broadcasted_iota(jnp.int32, (Q_BLOCK, 2 * CHUNK), 1)

    def score_pair(pp, carry):
        start = pl.multiple_of(pp * (2 * CHUNK), 2 * CHUNK)
        kx = kx_ref[pl.ds(start, 2 * CHUNK), :]
        acc = jnp.zeros((Q_BLOCK, 2 * CHUNK), F32)
        for h in range(IDX_HEADS):
            sc = _dot_nt(qx_ref[:, h * IDX_DIM:(h + 1) * IDX_DIM], kx)
            wb = wb_ref[h]
            acc = acc + jnp.maximum(sc, 0.0) * jnp.concatenate([wb, wb], axis=1)
        acc = jnp.where(start + cols2 <= q0 + rows2, acc, NEG_INF)
        bits = lax.bitcast_convert_type(acc, jnp.int32)
        keys = jnp.where(bits >= 0, bits, bits ^ jnp.int32(0x7FFFFFFF))
        key_ref[2 * pp] = keys[:, :CHUNK]
        key_ref[2 * pp + 1] = keys[:, CHUNK:]
        return carry

    lax.fori_loop(0, lax.shift_right_logical(n_ch + 1, 1), score_pair, 0)

    def count(pred):
        def body(ch, cnt):
            return cnt + jnp.where(pred(ch, key_ref[ch]), 1.0, 0.0)
        cnt = lax.fori_loop(0, n_ch, body, jnp.zeros((Q_BLOCK, CHUNK), F32))
        return jnp.sum(cnt, axis=1, keepdims=True)

    def unsettled(st):
        b, _, cnt_thr = st
        return (b < 32) & (jnp.max(jnp.abs(cnt_thr - kf)) > 0.5)

    def bisect(st):
        b, thr, cnt_thr = st
        cand = thr + lax.shift_left(jnp.int32(1), jnp.int32(31) - b)
        cand_b = jnp.broadcast_to(cand, (Q_BLOCK, CHUNK))
        cnt = count(lambda ch, key: key >= cand_b)
        ge = cnt >= kf
        return b + 1, jnp.where(ge, cand, thr), jnp.where(ge, cnt, cnt_thr)

    total = (n_ch * CHUNK).astype(F32)
    _, thr, cnt_thr = lax.while_loop(
        unsettled, bisect,
        (jnp.int32(0), jnp.full((Q_BLOCK, 1), INT_MIN, jnp.int32), jnp.full((Q_BLOCK, 1), total, F32)))
    thr_b = jnp.broadcast_to(thr, (Q_BLOCK, CHUNK))

    tie = cnt_thr > kf
    jp_ref[...] = jnp.full((Q_BLOCK, CHUNK), s_total, jnp.int32)

    @pl.when(jnp.max(jnp.where(tie, 1.0, 0.0)) > 0.5)
    def _():
        need = kf - count(lambda ch, key: key > thr_b)

        def refine(i, jp):
            cand = jp + lax.shift_left(jnp.int32(1), jnp.int32(s_total.bit_length() - 2) - i)
            cand_b = jnp.broadcast_to(cand, (Q_BLOCK, CHUNK))
            below = count(lambda ch, key: (key == thr_b) & (ch * CHUNK + cols < cand_b))
            return jnp.where(below < need, cand, jp)

        jp = lax.fori_loop(0, s_total.bit_length() - 1, refine, jnp.zeros((Q_BLOCK, 1), jnp.int32))
        jp_ref[...] = jnp.broadcast_to(jnp.where(tie, jp, s_total), (Q_BLOCK, CHUNK))

    jp_b = jp_ref[...]

    def write_chunk(ch, carry):
        key = key_ref[ch]
        pos = ch * CHUNK + cols
        keep = ((key > thr_b) | ((key == thr_b) & (pos <= jp_b))) & (pos <= q0 + rows)
        mask_ref[ch] = jnp.where(keep, 0.0, NEG_INF).astype(mask_ref.dtype)
        return carry

    lax.fori_loop(0, n_ch, write_chunk, 0)

    def blank_chunk(ch, carry):
        mask_ref[ch] = jnp.full((Q_BLOCK, CHUNK), NEG_INF, mask_ref.dtype)
        return carry

    lax.fori_loop(n_ch, nch_total, blank_chunk, 0)


def _dsa_index_mask(q_idx, w_idx_src, w_col_block, k_idx, top_k):
    s = q_idx.shape[0]
    nq, nch = s // Q_BLOCK, s // CHUNK
    w_scale = IDX_HEADS ** -0.5 * IDX_DIM ** -0.5
    return pl.pallas_call(
        functools.partial(_dsa_index_kernel, top_k=top_k, w_scale=w_scale),
        grid=(nq,),
        in_specs=[pl.BlockSpec((Q_BLOCK, IDX_HEADS * IDX_DIM), lambda i: (i, 0)),
                  pl.BlockSpec((Q_BLOCK, LANES), lambda i: (i, w_col_block)),
                  pl.BlockSpec((s, IDX_DIM), lambda i: (0, 0))],
        out_specs=pl.BlockSpec((None, nch, Q_BLOCK, CHUNK), lambda i: (i, 0, 0, 0)),
        out_shape=jax.ShapeDtypeStruct((nq, nch, Q_BLOCK, CHUNK), BF16),
        scratch_shapes=[pltpu.VMEM((nch, Q_BLOCK, CHUNK), jnp.int32),
                        pltpu.VMEM((IDX_HEADS, Q_BLOCK, LANES), F32),
                        pltpu.VMEM((Q_BLOCK, CHUNK), jnp.int32)],
        name="dsa_index",
        compiler_params=_cparams(("parallel",)),
    )(q_idx, w_idx_src, k_idx)


def _nsa_constants(s):
    ncp = s // CMP_STRIDE
    nsel = s // SEL_BLOCK
    c0 = np.arange(ncp) * CMP_STRIDE
    sel0 = np.arange(nsel) * SEL_BLOCK
    overlap = ((c0[:, None] < sel0[None, :] + SEL_BLOCK)
               & (c0[:, None] + CMP_LEN - 1 >= sel0[None, :])).astype(np.float32)
    expand_t = (np.arange(s)[:, None] // SEL_BLOCK == np.arange(nsel)[None, :]).astype(np.float32)
    return jnp.asarray(overlap, BF16), jnp.asarray(expand_t, BF16)


def _nsa_mixer(hn, h_res, bias_near, pat_cmp, w_in, pos_k, pos_v, w1_k, w2_k, w1_v, w2_v, w_out):
    s = hn.shape[0]
    g, d = NSA_KV_GROUPS, HEAD_DIM
    hd = N_HEADS * d
    q_scale = d ** -0.5 * LOG2E
    w_q = (w_in[:, :hd] * q_scale).astype(BF16)
    w_kv = w_in[:, hd:hd + 6 * g * d].astype(BF16)
    w_g = jnp.pad(w_in[:, hd + 6 * g * d:], ((0, 0), (0, LANES - 3 * N_HEADS))).astype(BF16)
    q = _matmul(hn, w_q, BF16)
    kv = _matmul(hn, w_kv, BF16)
    gate_logits = _matmul(hn, w_g, F32)
    gd = g * d
    kc, vc, ks, vs, kw, vw = [kv[:, i * gd:(i + 1) * gd] for i in range(6)]

    def rows16(a):
        return a.reshape(s, g, d).transpose(1, 0, 2).reshape(g, s // CMP_STRIDE, CMP_STRIDE * d)

    pos = jnp.stack([pos_k.reshape(2, CMP_STRIDE * d), pos_v.reshape(2, CMP_STRIDE * d)]).astype(F32)
    kv_cmp = _compress(jnp.stack([rows16(kc), rows16(vc)]), pos,
                       jnp.stack([w1_k, w1_v]).astype(BF16), jnp.stack([w2_k, w2_v]).astype(BF16))
    overlap, expand_t = _nsa_constants(s)
    o_c, sel = _nsa_compressed(q, kv_cmp, overlap, pat_cmp)
    o_s = _masked_attention(q, ks, vs, bias_near, mode="sel", nh=NSA_GROUP_HEADS, kv_shared=True,
                            sel=sel, expand_t=expand_t)
    o = _nsa_window_combine(q, kw, vw, bias_near, gate_logits, o_c, o_s)
    return _matmul(o, w_out.astype(BF16), F32, res=h_res)


def _dsa_mixer(hn, h_res, bias_near, w_in, g_q, g_kv, w_uq, w_uk, w_uv, w_qidx, ln_g, ln_b, w_out):
    s = hn.shape[0]
    d = HEAD_DIM
    n_in = w_in.shape[1]
    w_in_p = jnp.pad(w_in, ((0, 0), (0, _round_up(n_in, LANES) - n_in))).astype(BF16)
    proj = _matmul(hn, w_in_p, F32)
    c_q = _rownorm(proj, g_q, BF16, width=Q_RANK, col_block=0)
    c_kv = _rownorm(proj, g_kv, BF16, width=KV_RANK, col_block=Q_RANK // KV_RANK)
    k_idx = _rownorm(proj, ln_g, BF16, width=IDX_DIM, col_block=(Q_RANK + KV_RANK) // IDX_DIM, bias=ln_b)
    q_scale = d ** -0.5 * LOG2E
    q = _matmul(c_q, (w_uq * q_scale).astype(BF16), BF16)
    q_idx = _matmul(c_q, w_qidx.astype(BF16), BF16)
    w_k = w_uk.transpose(2, 0, 1).reshape(KV_RANK, N_HEADS * d).astype(BF16)
    w_v = w_uv.transpose(1, 0, 2).reshape(KV_RANK, N_HEADS * d).astype(BF16)
    k = _matmul(c_kv, w_k, BF16)
    v = _matmul(c_kv, w_v, BF16)
    top_k = min(IDX_TOPK, s // 4)
    mask = _dsa_index_mask(q_idx, proj, (Q_RANK + KV_RANK + IDX_DIM) // LANES, k_idx, top_k)
    o = _masked_attention(q, k, v, bias_near, mode="dsa", nh=4, kv_shared=False, mask=mask)
    return _matmul(o, w_out.astype(BF16), F32, res=h_res)


def _ffn(hn, h_res, w_gate, w_up, w_down):
    f = w_gate.shape[1]
    fp = _round_up(f, 1024) if f > 1024 else _round_up(f, LANES)
    wg = jnp.pad(w_gate.astype(BF16), ((0, 0), (0, fp - f)))
    wu = jnp.pad(w_up.astype(BF16), ((0, 0), (0, fp - f)))
    wd = jnp.pad(w_down.astype(BF16), ((0, fp - f), (0, 0)))
    return _matmul(_swiglu_up(hn, wg, wu), wd, F32, res=h_res)


def kernel(x, rel_bias, norm_mix, norm_ffn, norm_final, ffn_w_gate, ffn_w_up, ffn_w_down, nsa_w_in, nsa_cmp_pos_k, nsa_cmp_pos_v, nsa_cmp_w1_k, nsa_cmp_w2_k, nsa_cmp_w1_v, nsa_cmp_w2_v, nsa_w_out, dsa_w_in, dsa_norm_q, dsa_norm_kv, dsa_w_uq, dsa_w_uk, dsa_w_uv, dsa_w_qidx, dsa_idx_ln_g, dsa_idx_ln_b, dsa_w_out):
    b, s, dm = x.shape
    assert s % KEY_TILE == 0 and s // CMP_STRIDE >= LANES
    depth = norm_mix.shape[0]
    bias_near, pat_cmp = _bias_tiles(rel_bias)
    outs = []
    for bi in range(b):
        h = x[bi]
        for i in range(depth):
            hn = _rownorm(h, norm_mix[i], BF16)
            a = i // 2
            if i % 2 == 0:
                h = _nsa_mixer(hn, h, bias_near, pat_cmp, nsa_w_in[a], nsa_cmp_pos_k[a], nsa_cmp_pos_v[a],
                               nsa_cmp_w1_k[a], nsa_cmp_w2_k[a], nsa_cmp_w1_v[a], nsa_cmp_w2_v[a],
                               nsa_w_out[a])
            else:
                h = _dsa_mixer(hn, h, bias_near, dsa_w_in[a], dsa_norm_q[a], dsa_norm_kv[a], dsa_w_uq[a],
                               dsa_w_uk[a], dsa_w_uv[a], dsa_w_qidx[a], dsa_idx_ln_g[a],
                               dsa_idx_ln_b[a], dsa_w_out[a])
            hn = _rownorm(h, norm_ffn[i], BF16)
            h = _ffn(hn, h, ffn_w_gate[i], ffn_w_up[i], ffn_w_down[i])
        outs.append(_rownorm(h, norm_final, F32))
    return jnp.stack(outs)
```

```python
import functools
import math

import numpy as np
import jax
import jax.numpy as jnp
from jax import lax
from jax.experimental import pallas as pl
from jax.experimental.pallas import tpu as pltpu

N_HEADS = 32
HEAD_DIM = 128
NUM_BUCKETS = 32
MAX_DISTANCE = 128
Q_BLOCK = 128
NSA_KV_GROUPS = 4
NSA_GROUP_HEADS = N_HEADS // NSA_KV_GROUPS
CMP_LEN = 32
CMP_STRIDE = 16
SEL_BLOCK = 64
SEL_COUNT = 16
WINDOW = 512
FORCE_SCORE = 1e4
Q_RANK = 1024
KV_RANK = 512
IDX_HEADS = 32
IDX_DIM = 128
IDX_TOPK = 256
RMS_EPS = 1e-6

LANES = 128
V7X_VMEM_LIMIT_BYTES = 56 * 1024 * 1024
KEY_TILE = 512
CHUNK = 128

F32 = jnp.float32
BF16 = jnp.bfloat16
NEG_INF = float("-inf")
M_INIT = -1e30
MASK_BIG = 2.0 ** 100
LOG2E = math.log2(math.e)
INT_MIN = -(2 ** 31)


def _cparams(sem):
    return pltpu.CompilerParams(dimension_semantics=sem,
                                vmem_limit_bytes=V7X_VMEM_LIMIT_BYTES)


def _pick(n, candidates):
    for c in candidates:
        if n % c == 0:
            return c
    return n


def _round_up(n, m):
    return (n + m - 1) // m * m


def _rmsnorm_kernel(x_ref, g_ref, o_ref):
    x = x_ref[...].astype(F32)
    ms = jnp.mean(x * x, axis=-1, keepdims=True)
    o_ref[...] = (x * lax.rsqrt(ms + RMS_EPS) * g_ref[...]).astype(o_ref.dtype)


def _layernorm_kernel(x_ref, g_ref, b_ref, o_ref):
    x = x_ref[...].astype(F32)
    mu = jnp.mean(x, axis=-1, keepdims=True)
    xc = x - mu
    var = jnp.mean(xc * xc, axis=-1, keepdims=True)
    o_ref[...] = (xc * lax.rsqrt(var + RMS_EPS) * g_ref[...] + b_ref[...]).astype(o_ref.dtype)


def _rownorm(x, gain, out_dtype, *, width=None, col_block=0, bias=None):
    m = x.shape[0]
    width = x.shape[1] if width is None else width
    tm = _pick(m, (256, 128, 64, 32, 16, 8))
    row = lambda i: (i, col_block)
    vec = lambda i: (0, 0)
    g2 = gain.reshape(1, width).astype(F32)
    if bias is None:
        kern, extra, extra_specs = _rmsnorm_kernel, (), ()
    else:
        kern, extra = _layernorm_kernel, (bias.reshape(1, width).astype(F32),)
        extra_specs = (pl.BlockSpec((1, width), vec),)
    return pl.pallas_call(
        kern,
        grid=(m // tm,),
        in_specs=[pl.BlockSpec((tm, width), row), pl.BlockSpec((1, width), vec), *extra_specs],
        out_specs=pl.BlockSpec((tm, width), lambda i: (i, 0)),
        out_shape=jax.ShapeDtypeStruct((m, width), out_dtype),
        name="rownorm",
        compiler_params=_cparams(("parallel",)),
    )(x, g2, *extra)


def _mm_kernel(*refs, nk, has_res):
    a_ref, b_ref = refs[:2]
    r_ref = refs[2] if has_res else None
    o_ref = refs[3] if has_res else refs[2]
    acc_ref = refs[-1] if nk > 1 else None
    k = pl.program_id(2)
    part = jnp.dot(a_ref[...], b_ref[...], preferred_element_type=F32)

    def finish(acc):
        if has_res:
            acc = acc + r_ref[...]
        o_ref[...] = acc.astype(o_ref.dtype)

    if nk == 1:
        finish(part)
        return

    @pl.when(k == 0)
    def _():
        acc_ref[...] = part

    @pl.when((k > 0) & (k < nk - 1))
    def _():
        acc_ref[...] += part

    @pl.when(k == nk - 1)
    def _():
        finish(acc_ref[...] + part)


MAX_FULL_K = 4096


def _matmul(a, b, out_dtype, res=None):
    m, kd = a.shape
    n = b.shape[1]
    tm = _pick(m, (1024, 512, 256, 128))
    if kd <= MAX_FULL_K:
        tk = kd
        tn_cap = min(2048, max(512, (2 * 1024 * 1024) // kd))
        tn = _pick(n, [c for c in (2048, 1024, 896, 512, 384, 256, 128) if c <= max(tn_cap, 896)])
    else:
        tk = _pick(kd, (2816, 2048, 1024, 512, 256, 128))
        tn = _pick(n, (1024, 512, 256, 128))
    nk = kd // tk
    in_specs = [pl.BlockSpec((tm, tk), lambda i, j, k: (i, k)),
                pl.BlockSpec((tk, tn), lambda i, j, k: (k, j))]
    args = [a, b]
    if res is not None:
        in_specs.append(pl.BlockSpec((tm, tn), lambda i, j, k: (i, j)))
        args.append(res)
    return pl.pallas_call(
        functools.partial(_mm_kernel, nk=nk, has_res=res is not None),
        grid=(m // tm, n // tn, nk),
        in_specs=in_specs,
        out_specs=pl.BlockSpec((tm, tn), lambda i, j, k: (i, j)),
        out_shape=jax.ShapeDtypeStruct((m, n), out_dtype),
        scratch_shapes=[pltpu.VMEM((tm, tn), F32)] if nk > 1 else [],
        name="matmul",
        compiler_params=_cparams(("parallel", "parallel", "arbitrary")),
    )(*args)


def _swiglu_kernel(x_ref, wg_ref, wu_ref, o_ref, *, n_real):
    j = pl.program_id(1)

    @pl.when(j < n_real)
    def _():
        x = x_ref[...]
        g = jnp.dot(x, wg_ref[...].astype(BF16), preferred_element_type=F32)
        u = jnp.dot(x, wu_ref[...].astype(BF16), preferred_element_type=F32)
        o_ref[...] = (g * jax.nn.sigmoid(g) * u).astype(o_ref.dtype)

    @pl.when(j >= n_real)
    def _():
        o_ref[...] = jnp.zeros_like(o_ref)


SWIGLU_TN = 256


def _swiglu_up(x, wg, wu, n_out):
    m, kd = x.shape
    n = wg.shape[1]
    tn = _pick(n, (SWIGLU_TN, 128))
    assert kd <= MAX_FULL_K and n % tn == 0 and n_out % tn == 0
    n_real = n // tn
    tm = _pick(m, (2048, 1024, 512, 256, 128))
    w_spec = pl.BlockSpec((kd, tn), lambda i, j: (0, jnp.minimum(j, n_real - 1)))
    return pl.pallas_call(
        functools.partial(_swiglu_kernel, n_real=n_real),
        grid=(m // tm, n_out // tn),
        in_specs=[pl.BlockSpec((tm, kd), lambda i, j: (i, 0), pipeline_mode=pl.Buffered(1)),
                  w_spec, w_spec],
        out_specs=pl.BlockSpec((tm, tn), lambda i, j: (i, j)),
        out_shape=jax.ShapeDtypeStruct((m, n_out), BF16),
        name="swiglu_up",
        compiler_params=_cparams(("parallel", "arbitrary")),
    )(x, wg, wu)


def _dot_nt(a, b):
    return lax.dot_general(a, b, (((1,), (1,)), ((), ())), preferred_element_type=F32)


def _with_ones(v):
    return jnp.concatenate([v, jnp.ones_like(v)], axis=1)


def _lane_blocks(x):
    return [x[:, c * LANES:(c + 1) * LANES] for c in range(x.shape[1] // LANES)]


def _softmax_step(q, k, v_ones, add, m_ref, acc_ref, h):
    z = _dot_nt(q, k)
    if add is not None:
        z = z + add
    m_old = m_ref[h]
    m_new = jnp.maximum(m_old, jnp.max(z, axis=1, keepdims=True))
    alpha = jnp.exp2(m_old - m_new)
    p = jnp.concatenate([jnp.exp2(zc - m_new) for zc in _lane_blocks(z)], axis=1).astype(BF16)
    pv = jnp.dot(p, v_ones, preferred_element_type=F32)
    acc_ref[h] = jnp.concatenate([alpha, alpha], axis=1) * acc_ref[h] + pv
    m_ref[h] = m_new


def _softmax_init(m_ref, acc_ref):
    m_ref[...] = jnp.full_like(m_ref, M_INIT)
    acc_ref[...] = jnp.zeros_like(acc_ref)


def _normalized(acc):
    return acc[:, :HEAD_DIM] * (1.0 / jnp.maximum(acc[:, HEAD_DIM:], 1e-30))


def _attend_once(q, k, v_ones, add):
    z = _dot_nt(q, k) + add
    m = jnp.maximum(jnp.max(z, axis=1, keepdims=True), M_INIT)
    p = jnp.exp2(z - m).astype(BF16)
    return _normalized(jnp.dot(p, v_ones, preferred_element_type=F32))


def _t5_bucket(dist):
    n = jnp.maximum(dist, 0)
    max_exact = NUM_BUCKETS // 2
    nf = jnp.maximum(n, 1).astype(F32)
    large = max_exact + (jnp.log(nf / max_exact) / math.log(MAX_DISTANCE / max_exact)
                         * (NUM_BUCKETS - max_exact)).astype(jnp.int32)
    large = jnp.minimum(large, NUM_BUCKETS - 1)
    return jnp.where(n < max_exact, n, large)


def _bias_tiles(rel_bias):
    tab = (rel_bias.astype(F32).T - rel_bias.astype(F32)[NUM_BUCKETS - 1][:, None]) * LOG2E

    def lookup(bucket):
        out = jnp.zeros((tab.shape[0],) + bucket.shape, F32)
        for b in range(NUM_BUCKETS):
            out = jnp.where(bucket[None] == b, tab[:, b][:, None, None], out)
        return out

    i = np.arange(Q_BLOCK)[:, None]
    j = np.arange(2 * CHUNK)[None, :]
    dist_near = jnp.asarray(np.where(j < CHUNK, CHUNK + i - j, i - (j - CHUNK)), jnp.int32)
    near = jnp.where(dist_near >= 0, lookup(_t5_bucket(dist_near)), NEG_INF)
    lane = np.arange(LANES)[None, :]
    dist_cmp = jnp.asarray(i - CMP_STRIDE * (lane - LANES // 2) - (CMP_LEN - 1), jnp.int32)
    cmp_ = jnp.where(dist_cmp >= 0, lookup(_t5_bucket(dist_cmp)), 0.0)
    return near, cmp_


def _compress_kernel(x_ref, pos_ref, w1_ref, w2_ref, o_ref, *, half):
    x = x_ref[...].astype(F32)
    xa = (x + pos_ref[0:1, :]).astype(BF16)
    xb = (x + pos_ref[1:2, :]).astype(BF16)
    a = jnp.dot(xa, w1_ref[0:half, :], preferred_element_type=F32)
    b = jnp.dot(xb, w1_ref[half:2 * half, :], preferred_element_type=F32)
    nc = a.shape[0]
    pre = a + pltpu.roll(b, nc - 1, 0)
    hmid = pre * jax.nn.sigmoid(pre)
    o_ref[...] = jnp.dot(hmid.astype(BF16), w2_ref[...], preferred_element_type=F32).astype(o_ref.dtype)


def _compress(kv_rows, pos, w1, w2):
    two, g, nc, wd = kv_rows.shape
    d = w2.shape[-1]
    return pl.pallas_call(
        functools.partial(_compress_kernel, half=wd),
        grid=(two, g),
        in_specs=[pl.BlockSpec((None, None, nc, wd), lambda a, b: (a, b, 0, 0)),
                  pl.BlockSpec((None, 2, wd), lambda a, b: (a, 0, 0)),
                  pl.BlockSpec((None, 2 * wd, d), lambda a, b: (a, 0, 0)),
                  pl.BlockSpec((None, d, d), lambda a, b: (a, 0, 0))],
        out_specs=pl.BlockSpec((None, None, nc, d), lambda a, b: (a, b, 0, 0)),
        out_shape=jax.ShapeDtypeStruct((two, g, nc, d), BF16),
        name="nsa_compress",
        compiler_params=_cparams(("parallel", "parallel")),
    )(kv_rows, pos, w1, w2)


def _nsa_cmp_kernel(q_ref, kc_ref, vc_ref, ov_ref, pat_ref, o_ref, imp_ref):
    qi = pl.program_id(1)
    q0 = qi * Q_BLOCK
    ncp = kc_ref.shape[0]
    nsel = ov_ref.shape[1]
    rows = lax.broadcasted_iota(jnp.int32, (Q_BLOCK, ncp), 0)
    cols = lax.broadcasted_iota(jnp.int32, (Q_BLOCK, ncp), 1)
    add_mask = jnp.where(cols * CMP_STRIDE + (CMP_LEN - 1) <= q0 + rows, 0.0, NEG_INF)
    shift = (qi * (Q_BLOCK // CMP_STRIDE) + (ncp - LANES // 2)) % ncp
    kc = kc_ref[...]
    rhs = jnp.concatenate([vc_ref[...], ov_ref[...], jnp.ones((ncp, LANES), BF16)], axis=1)
    imp = jnp.zeros((Q_BLOCK, nsel), F32)
    for r in range(NSA_GROUP_HEADS):
        pat = pat_ref[r]
        if ncp > LANES:
            pat = jnp.concatenate([pat, jnp.zeros((Q_BLOCK, ncp - LANES), F32)], axis=1)
        bias = pltpu.roll(pat, shift, 1)
        z = _dot_nt(q_ref[:, r * HEAD_DIM:(r + 1) * HEAD_DIM], kc) + bias + add_mask
        m = jnp.maximum(jnp.max(z, axis=1, keepdims=True), M_INIT)
        e = jnp.exp2(z - m).astype(BF16)
        acc = jnp.dot(e, rhs, preferred_element_type=F32)
        inv = 1.0 / jnp.maximum(acc[:, HEAD_DIM + nsel:], 1e-30)
        o_ref[:, r * HEAD_DIM:(r + 1) * HEAD_DIM] = (acc[:, :HEAD_DIM] * inv).astype(o_ref.dtype)
        imp = imp + acc[:, HEAD_DIM:HEAD_DIM + nsel] * inv[:, :nsel]
    imp_ref[...] = imp


def _nsa_select_kernel(imp_ref, msel_ref, *, n_top):
    qi = pl.program_id(0)
    ng, _, nsel = imp_ref.shape
    t = qi * Q_BLOCK + lax.broadcasted_iota(jnp.int32, (Q_BLOCK, nsel), 0)
    j = lax.broadcasted_iota(jnp.int32, (Q_BLOCK, nsel), 1)
    blk_t = t // SEL_BLOCK
    forced = (j == 0) | (j == blk_t) | (j == blk_t - 1)
    visible = j * SEL_BLOCK <= t
    jf = j.astype(F32)
    imps = tuple(jnp.where(visible, jnp.where(forced, FORCE_SCORE, imp_ref[g]), -1.0) for g in range(ng))
    sels = tuple(jnp.zeros((Q_BLOCK, nsel), F32) for _ in range(ng))

    def pick(_, carry):
        out_i, out_s = [], []
        for imp_c, sel_c in zip(*carry):
            best = jnp.max(imp_c, axis=1, keepdims=True)
            first = jnp.min(jnp.where(imp_c == best, jf, float(nsel)), axis=1, keepdims=True)
            hit = jf == first
            out_i.append(jnp.where(hit, NEG_INF, imp_c))
            out_s.append(jnp.where(hit, 1.0, sel_c))
        return tuple(out_i), tuple(out_s)

    _, sels = lax.fori_loop(0, n_top, pick, (imps, sels))
    for g in range(ng):
        msel_ref[g] = jnp.where(sels[g] > 0.5, 0.0, -MASK_BIG).astype(msel_ref.dtype)


def _nsa_select(imp):
    g, s, nsel = imp.shape
    blk = pl.BlockSpec((g, Q_BLOCK, nsel), lambda i: (0, i, 0))
    return pl.pallas_call(
        functools.partial(_nsa_select_kernel, n_top=min(SEL_COUNT, nsel)),
        grid=(s // Q_BLOCK,),
        in_specs=[blk],
        out_specs=blk,
        out_shape=jax.ShapeDtypeStruct((g, s, nsel), BF16),
        name="nsa_select",
        compiler_params=_cparams(("parallel",)),
    )(imp)


def _nsa_compressed(q, kv_cmp, overlap, pat_cmp):
    s = q.shape[0]
    g, ncp, d = kv_cmp.shape[1:]
    nsel = overlap.shape[1]
    gw = NSA_GROUP_HEADS * HEAD_DIM
    return pl.pallas_call(
        _nsa_cmp_kernel,
        grid=(g, s // Q_BLOCK),
        in_specs=[pl.BlockSpec((Q_BLOCK, gw), lambda a, b: (b, a)),
                  pl.BlockSpec((None, None, ncp, d), lambda a, b: (0, a, 0, 0)),
                  pl.BlockSpec((None, None, ncp, d), lambda a, b: (1, a, 0, 0)),
                  pl.BlockSpec((ncp, nsel), lambda a, b: (0, 0)),
                  pl.BlockSpec((NSA_GROUP_HEADS, Q_BLOCK, LANES), lambda a, b: (a, 0, 0))],
        out_specs=[pl.BlockSpec((Q_BLOCK, gw), lambda a, b: (b, a)),
                   pl.BlockSpec((None, Q_BLOCK, nsel), lambda a, b: (a, b, 0))],
        out_shape=[jax.ShapeDtypeStruct((s, N_HEADS * HEAD_DIM), BF16),
                   jax.ShapeDtypeStruct((g, s, nsel), F32)],
        name="nsa_cmp_attn",
        compiler_params=_cparams(("parallel", "parallel")),
    )(q, kv_cmp, kv_cmp, overlap, pat_cmp)


def _masked_attn_kernel(*refs, nh, kv_shared, mode):
    if mode == "sel":
        q_ref, k_ref, v_ref, bias_ref, msel_ref, et_ref, o_ref, m_ref, acc_ref = refs
    else:
        q_ref, k_ref, v_ref, bias_ref, mask_ref, o_ref, m_ref, acc_ref = refs
    qi = pl.program_id(1)
    q0 = qi * Q_BLOCK
    far_end = q0 - CHUNK
    _softmax_init(m_ref, acc_ref)
    tile_chunks = KEY_TILE // CHUNK

    def head_q(h):
        return q_ref[:, h * HEAD_DIM:(h + 1) * HEAD_DIM]

    def head_k(rows, h):
        return k_ref[rows, :] if kv_shared else k_ref[rows, h * HEAD_DIM:(h + 1) * HEAD_DIM]

    def head_v(rows, h):
        return v_ref[rows, :] if kv_shared else v_ref[rows, h * HEAD_DIM:(h + 1) * HEAD_DIM]

    def mask_chunks(c0, n):
        if mode == "sel":
            return _dot_nt(msel_ref[...], et_ref[pl.ds(pl.multiple_of(c0 * CHUNK, CHUNK), n * CHUNK), :])
        parts = [mask_ref[c0 + c].astype(F32) for c in range(n)]
        return parts[0] if n == 1 else jnp.concatenate(parts, axis=1)

    def far_tile(kt, carry):
        start = pl.multiple_of(kt * KEY_TILE, KEY_TILE)
        cols = start + lax.broadcasted_iota(jnp.int32, (Q_BLOCK, KEY_TILE), 1)
        add = jnp.where(cols < far_end, mask_chunks(kt * tile_chunks, tile_chunks), NEG_INF)
        rows = pl.ds(start, KEY_TILE)
        for h in range(nh):
            _softmax_step(head_q(h), head_k(rows, h), _with_ones(head_v(rows, h)), add, m_ref, acc_ref, h)
        return carry

    n_far = lax.shift_right_logical(jnp.maximum(far_end, 0) + KEY_TILE - 1, KEY_TILE.bit_length() - 1)
    lax.fori_loop(0, n_far, far_tile, 0)

    prev0 = pl.multiple_of(jnp.maximum(far_end, 0), CHUNK)
    diag0 = pl.multiple_of(q0, CHUNK)
    no_prev = jnp.where(qi > 0, 0.0, NEG_INF)
    add_prev = mask_chunks(jnp.maximum(qi - 1, 0), 1) + no_prev
    add_diag = mask_chunks(qi, 1)
    add_near = jnp.concatenate([add_prev, add_diag], axis=1)
    for h in range(nh):
        k = jnp.concatenate([head_k(pl.ds(prev0, CHUNK), h), head_k(pl.ds(diag0, CHUNK), h)], axis=0)
        v = jnp.concatenate([head_v(pl.ds(prev0, CHUNK), h), head_v(pl.ds(diag0, CHUNK), h)], axis=0)
        _softmax_step(head_q(h), k, _with_ones(v), add_near + bias_ref[h], m_ref, acc_ref, h)
    for h in range(nh):
        o_ref[:, h * HEAD_DIM:(h + 1) * HEAD_DIM] = _normalized(acc_ref[h]).astype(o_ref.dtype)


def _masked_attention(q, k, v, bias_near, *, mode, nh, k_col0=0, v_col0=0, msel=None, expand_t=None,
                      mask=None):
    s = q.shape[0]
    n_groups = N_HEADS // nh
    qw = nh * HEAD_DIM
    kv_shared = mode == "sel"
    q_spec = pl.BlockSpec((Q_BLOCK, qw), lambda a, b: (b, a))
    bias_spec = pl.BlockSpec((nh, Q_BLOCK, 2 * CHUNK), lambda a, b: (a, 0, 0))
    if mode == "sel":
        nsel = msel.shape[-1]
        in_specs = [q_spec,
                    pl.BlockSpec((s, HEAD_DIM), lambda a, b: (0, k_col0 + a)),
                    pl.BlockSpec((s, HEAD_DIM), lambda a, b: (0, v_col0 + a)),
                    bias_spec,
                    pl.BlockSpec((None, Q_BLOCK, nsel), lambda a, b: (a, b, 0)),
                    pl.BlockSpec((s, nsel), lambda a, b: (0, 0))]
        args = [q, k, v, bias_near, msel, expand_t]
    else:
        nch = s // CHUNK
        once = pl.Buffered(1)
        in_specs = [q_spec,
                    pl.BlockSpec((s, qw), lambda a, b: (0, a), pipeline_mode=once),
                    pl.BlockSpec((s, qw), lambda a, b: (0, a), pipeline_mode=once),
                    bias_spec,
                    pl.BlockSpec((None, nch, Q_BLOCK, CHUNK), lambda a, b: (b, 0, 0, 0))]
        args = [q, k, v, bias_near, mask]
    return pl.pallas_call(
        functools.partial(_masked_attn_kernel, nh=nh, kv_shared=kv_shared, mode=mode),
        grid=(n_groups, s // Q_BLOCK),
        in_specs=in_specs,
        out_specs=pl.BlockSpec((Q_BLOCK, qw), lambda a, b: (b, a)),
        out_shape=jax.ShapeDtypeStruct((s, N_HEADS * HEAD_DIM), BF16),
        scratch_shapes=[pltpu.VMEM((nh, Q_BLOCK, LANES), F32),
                        pltpu.VMEM((nh, Q_BLOCK, 2 * HEAD_DIM), F32)],
        name="masked_attn_" + mode,
        compiler_params=_cparams(("parallel", "arbitrary")),
    )(*args)


def _nsa_win_kernel(q_ref, k_ref, v_ref, bias_ref, gl_ref, oc_ref, os_ref, o_ref, *, n_far):
    g = pl.program_id(0)
    qi = pl.program_id(1)
    q0 = qi * Q_BLOCK
    rows = lax.broadcasted_iota(jnp.int32, (Q_BLOCK, CHUNK), 0)
    cols = lax.broadcasted_iota(jnp.int32, (Q_BLOCK, CHUNK), 1)
    adds, ks, vs = [], [], []
    for c in range(n_far + 2):
        back = n_far + 1 - c
        start = pl.multiple_of(jnp.maximum(q0 - back * CHUNK, 0), CHUNK)
        valid = qi >= back
        if c == 0:
            ok = valid & (cols > rows)
        else:
            ok = jnp.broadcast_to(valid, (Q_BLOCK, CHUNK))
        adds.append(jnp.where(ok, 0.0, NEG_INF))
        ks.append(k_ref[pl.ds(start, CHUNK), :])
        vs.append(v_ref[pl.ds(start, CHUNK), :])
    k_all = jnp.concatenate(ks, axis=0)
    v_all = _with_ones(jnp.concatenate(vs, axis=0))
    add_far = jnp.concatenate(adds[:n_far], axis=1)
    add_near = jnp.concatenate(adds[n_far:], axis=1)
    gates = jax.nn.sigmoid(gl_ref[...])
    lane = lax.broadcasted_iota(jnp.int32, (Q_BLOCK, LANES), 1)
    for h in range(NSA_GROUP_HEADS):
        sl = slice(h * HEAD_DIM, (h + 1) * HEAD_DIM)
        head = g * NSA_GROUP_HEADS + h
        gate = [jnp.sum(jnp.where(lane == br * N_HEADS + head, gates, 0.0), axis=1, keepdims=True)
                for br in range(3)]
        add = jnp.concatenate([add_far, add_near + bias_ref[h]], axis=1)
        o_w = _attend_once(q_ref[:, sl], k_all, v_all, add)
        o = (gate[0] * oc_ref[:, sl].astype(F32) + gate[1] * os_ref[:, sl].astype(F32)
             + gate[2] * o_w)
        o_ref[:, sl] = o.astype(o_ref.dtype)


def _nsa_window_combine(q, kv, k_col0, v_col0, bias_near, gate_logits, o_c, o_s):
    s = q.shape[0]
    g = NSA_KV_GROUPS
    gw = NSA_GROUP_HEADS * HEAD_DIM
    n_far = WINDOW // CHUNK - 1
    blk = pl.BlockSpec((Q_BLOCK, gw), lambda a, b: (b, a))
    return pl.pallas_call(
        functools.partial(_nsa_win_kernel, n_far=n_far),
        grid=(g, s // Q_BLOCK),
        in_specs=[blk,
                  pl.BlockSpec((s, HEAD_DIM), lambda a, b: (0, k_col0 + a)),
                  pl.BlockSpec((s, HEAD_DIM), lambda a, b: (0, v_col0 + a)),
                  pl.BlockSpec((NSA_GROUP_HEADS, Q_BLOCK, 2 * CHUNK), lambda a, b: (a, 0, 0)),
                  pl.BlockSpec((Q_BLOCK, LANES), lambda a, b: (b, 0)),
                  blk, blk],
        out_specs=blk,
        out_shape=jax.ShapeDtypeStruct((s, N_HEADS * HEAD_DIM), BF16),
        name="nsa_window",
        compiler_params=_cparams(("parallel", "parallel")),
    )(q, kv, kv, bias_near, gate_logits, o_c, o_s)


def _dsa_index_kernel(qx_ref, w_ref, kx_ref, mask_ref, key_ref, wb_ref, jp_ref, *, top_k, w_scale):
    qi = pl.program_id(0)
    q0 = qi * Q_BLOCK
    n_ch = qi + 1
    nch_total = key_ref.shape[0]
    s_total = nch_total * CHUNK
    kf = float(top_k)
    w = w_ref[...] * w_scale
    lane = lax.broadcasted_iota(jnp.int32, (Q_BLOCK, LANES), 1)
    for h in range(IDX_HEADS):
        col = jnp.sum(jnp.where(lane == h, w, 0.0), axis=1, keepdims=True)
        wb_ref[h] = jnp.broadcast_to(col, (Q_BLOCK, LANES))
    rows = lax.broadcasted_iota(jnp.int32, (Q_BLOCK, CHUNK), 0)
    cols = lax.broadcasted_iota(jnp.int32, (Q_BLOCK, CHUNK), 1)
    rows2 = lax.broadcasted_iota(jnp.int32, (Q_BLOCK, 2 * CHUNK), 0)
    cols2 = lax.broadcasted_iota(jnp.int32, (Q_BLOCK, 2 * CHUNK), 1)

    def score_pair(pp, carry):
        start = pl.multiple_of(pp * (2 * CHUNK), 2 * CHUNK)
        kx = kx_ref[pl.ds(start, 2 * CHUNK), :]
        acc = jnp.zeros((Q_BLOCK, 2 * CHUNK), F32)
        for h in range(IDX_HEADS):
            sc = _dot_nt(qx_ref[:, h * IDX_DIM:(h + 1) * IDX_DIM], kx)
            wb = wb_ref[h]
            acc = acc + jnp.maximum(sc, 0.0) * jnp.concatenate([wb, wb], axis=1)
        acc = jnp.where(start + cols2 <= q0 + rows2, acc, NEG_INF)
        bits = lax.bitcast_convert_type(acc, jnp.int32)
        keys = jnp.where(bits >= 0, bits, bits ^ jnp.int32(0x7FFFFFFF))
        key_ref[2 * pp] = keys[:, :CHUNK]
        key_ref[2 * pp + 1] = keys[:, CHUNK:]
        return carry

    lax.fori_loop(0, lax.shift_right_logical(n_ch + 1, 1), score_pair, 0)

    def count(pred):
        def body(ch, cnt):
            return cnt + jnp.where(pred(ch, key_ref[ch]), 1.0, 0.0)
        cnt = lax.fori_loop(0, n_ch, body, jnp.zeros((Q_BLOCK, CHUNK), F32))
        return jnp.sum(cnt, axis=1, keepdims=True)

    def unsettled(st):
        b, _, cnt_thr = st
        return (b < 32) & (jnp.max(jnp.abs(cnt_thr - kf)) > 0.5)

    def bisect(st):
        b, thr, cnt_thr = st
        cand = thr + lax.shift_left(jnp.int32(1), jnp.int32(31) - b)
        cand_b = jnp.broadcast_to(cand, (Q_BLOCK, CHUNK))
        cnt = count(lambda ch, key: key >= cand_b)
        ge = cnt >= kf
        return b + 1, jnp.where(ge, cand, thr), jnp.where(ge, cnt, cnt_thr)

    total = (n_ch * CHUNK).astype(F32)
    _, thr, cnt_thr = lax.while_loop(
        unsettled, bisect,
        (jnp.int32(0), jnp.full((Q_BLOCK, 1), INT_MIN, jnp.int32), jnp.full((Q_BLOCK, 1), total, F32)))
    thr_b = jnp.broadcast_to(thr, (Q_BLOCK, CHUNK))

    tie = cnt_thr > kf
    jp_ref[...] = jnp.full((Q_BLOCK, CHUNK), s_total, jnp.int32)

    @pl.when(jnp.max(jnp.where(tie, 1.0, 0.0)) > 0.5)
    def _():
        need = kf - count(lambda ch, key: key > thr_b)

        def refine(i, jp):
            cand = jp + lax.shift_left(jnp.int32(1), jnp.int32(s_total.bit_length() - 2) - i)
            cand_b = jnp.broadcast_to(cand, (Q_BLOCK, CHUNK))
            below = count(lambda ch, key: (key == thr_b) & (ch * CHUNK + cols < cand_b))
            return jnp.where(below < need, cand, jp)

        jp = lax.fori_loop(0, s_total.bit_length() - 1, refine, jnp.zeros((Q_BLOCK, 1), jnp.int32))
        jp_ref[...] = jnp.broadcast_to(jnp.where(tie, jp, s_total), (Q_BLOCK, CHUNK))

    jp_b = jp_ref[...]

    def write_chunk(ch, carry):
        key = key_ref[ch]
        pos = ch * CHUNK + cols
        keep = ((key > thr_b) | ((key == thr_b) & (pos <= jp_b))) & (pos <= q0 + rows)
        mask_ref[ch] = jnp.where(keep, 0.0, NEG_INF).astype(mask_ref.dtype)
        return carry

    lax.fori_loop(0, n_ch, write_chunk, 0)

    def blank_chunk(ch, carry):
        mask_ref[ch] = jnp.full((Q_BLOCK, CHUNK), NEG_INF, mask_ref.dtype)
        return carry

    lax.fori_loop(n_ch, nch_total, blank_chunk, 0)


def _dsa_index_mask(q_idx, w_idx_src, w_col_block, k_idx, top_k):
    s = q_idx.shape[0]
    nq, nch = s // Q_BLOCK, s // CHUNK
    w_scale = IDX_HEADS ** -0.5 * IDX_DIM ** -0.5
    return pl.pallas_call(
        functools.partial(_dsa_index_kernel, top_k=top_k, w_scale=w_scale),
        grid=(nq,),
        in_specs=[pl.BlockSpec((Q_BLOCK, IDX_HEADS * IDX_DIM), lambda i: (i, 0)),
                  pl.BlockSpec((Q_BLOCK, LANES), lambda i: (i, w_col_block)),
                  pl.BlockSpec((s, IDX_DIM), lambda i: (0, 0))],
        out_specs=pl.BlockSpec((None, nch, Q_BLOCK, CHUNK), lambda i: (i, 0, 0, 0)),
        out_shape=jax.ShapeDtypeStruct((nq, nch, Q_BLOCK, CHUNK), BF16),
        scratch_shapes=[pltpu.VMEM((nch, Q_BLOCK, CHUNK), jnp.int32),
                        pltpu.VMEM((IDX_HEADS, Q_BLOCK, LANES), F32),
                        pltpu.VMEM((Q_BLOCK, CHUNK), jnp.int32)],
        name="dsa_index",
        compiler_params=_cparams(("parallel",)),
    )(q_idx, w_idx_src, k_idx)


def _nsa_constants(s):
    ncp = s // CMP_STRIDE
    nsel = s // SEL_BLOCK
    c0 = np.arange(ncp) * CMP_STRIDE
    sel0 = np.arange(nsel) * SEL_BLOCK
    overlap = ((c0[:, None] < sel0[None, :] + SEL_BLOCK)
               & (c0[:, None] + CMP_LEN - 1 >= sel0[None, :])).astype(np.float32)
    expand_t = (np.arange(s)[:, None] // SEL_BLOCK == np.arange(nsel)[None, :]).astype(np.float32)
    return jnp.asarray(overlap, BF16), jnp.asarray(expand_t, BF16)


def _nsa_mixer(hn, h_res, bias_near, pat_cmp, w_in, pos_k, pos_v, w1_k, w2_k, w1_v, w2_v, w_out):
    s = hn.shape[0]
    g, d = NSA_KV_GROUPS, HEAD_DIM
    hd = N_HEADS * d
    q_scale = d ** -0.5 * LOG2E
    w_q = (w_in[:, :hd] * q_scale).astype(BF16)
    w_kv = w_in[:, hd:hd + 6 * g * d].astype(BF16)
    w_g = jnp.pad(w_in[:, hd + 6 * g * d:], ((0, 0), (0, LANES - 3 * N_HEADS))).astype(BF16)
    q = _matmul(hn, w_q, BF16)
    kv = _matmul(hn, w_kv, BF16)
    gate_logits = _matmul(hn, w_g, F32)
    kv6 = kv.reshape(s, 6, g, d)
    rows16 = kv6[:, :2].transpose(1, 2, 0, 3).reshape(2, g, s // CMP_STRIDE, CMP_STRIDE * d)
    pos = jnp.stack([pos_k.reshape(2, CMP_STRIDE * d), pos_v.reshape(2, CMP_STRIDE * d)]).astype(F32)
    kv_cmp = _compress(rows16, pos,
                       jnp.stack([w1_k, w1_v]).astype(BF16), jnp.stack([w2_k, w2_v]).astype(BF16))
    overlap, expand_t = _nsa_constants(s)
    o_c, imp = _nsa_compressed(q, kv_cmp, overlap, pat_cmp)
    msel = _nsa_select(imp)
    o_s = _masked_attention(q, kv, kv, bias_near, mode="sel", nh=NSA_GROUP_HEADS, k_col0=2 * g, v_col0=3 * g,
                            msel=msel, expand_t=expand_t)
    o = _nsa_window_combine(q, kv, 4 * g, 5 * g, bias_near, gate_logits, o_c, o_s)
    return _matmul(o, w_out.astype(BF16), F32, res=h_res)


def _dsa_mixer(hn, h_res, bias_near, w_in, g_q, g_kv, w_uq, w_uk, w_uv, w_qidx, ln_g, ln_b, w_out):
    s = hn.shape[0]
    d = HEAD_DIM
    n_in = w_in.shape[1]
    w_in_p = jnp.pad(w_in, ((0, 0), (0, _round_up(n_in, LANES) - n_in))).astype(BF16)
    proj = _matmul(hn, w_in_p, F32)
    c_q = _rownorm(proj, g_q, BF16, width=Q_RANK, col_block=0)
    c_kv = _rownorm(proj, g_kv, BF16, width=KV_RANK, col_block=Q_RANK // KV_RANK)
    k_idx = _rownorm(proj, ln_g, BF16, width=IDX_DIM, col_block=(Q_RANK + KV_RANK) // IDX_DIM, bias=ln_b)
    q_scale = d ** -0.5 * LOG2E
    q = _matmul(c_q, (w_uq * q_scale).astype(BF16), BF16)
    q_idx = _matmul(c_q, w_qidx.astype(BF16), BF16)
    w_k = w_uk.transpose(2, 0, 1).reshape(KV_RANK, N_HEADS * d).astype(BF16)
    w_v = w_uv.transpose(1, 0, 2).reshape(KV_RANK, N_HEADS * d).astype(BF16)
    k = _matmul(c_kv, w_k, BF16)
    v = _matmul(c_kv, w_v, BF16)
    top_k = min(IDX_TOPK, s // 4)
    mask = _dsa_index_mask(q_idx, proj, (Q_RANK + KV_RANK + IDX_DIM) // LANES, k_idx, top_k)
    o = _masked_attention(q, k, v, bias_near, mode="dsa", nh=8, mask=mask)
    return _matmul(o, w_out.astype(BF16), F32, res=h_res)


def _ffn(hn, h_res, w_gate, w_up, w_down):
    f = w_gate.shape[1]
    fp = _round_up(f, 1024) if f > 1024 else _round_up(f, LANES)
    wd = jnp.pad(w_down.astype(BF16), ((0, fp - f), (0, 0)))
    return _matmul(_swiglu_up(hn, w_gate, w_up, fp), wd, F32, res=h_res)


def kernel(x, rel_bias, norm_mix, norm_ffn, norm_final, ffn_w_gate, ffn_w_up, ffn_w_down, nsa_w_in, nsa_cmp_pos_k, nsa_cmp_pos_v, nsa_cmp_w1_k, nsa_cmp_w2_k, nsa_cmp_w1_v, nsa_cmp_w2_v, nsa_w_out, dsa_w_in, dsa_norm_q, dsa_norm_kv, dsa_w_uq, dsa_w_uk, dsa_w_uv, dsa_w_qidx, dsa_idx_ln_g, dsa_idx_ln_b, dsa_w_out):
    b, s, dm = x.shape
    assert s % KEY_TILE == 0 and s // CMP_STRIDE >= LANES
    depth = norm_mix.shape[0]
    bias_near, pat_cmp = _bias_tiles(rel_bias)
    outs = []
    for bi in range(b):
        h = x[bi]
        for i in range(depth):
            hn = _rownorm(h, norm_mix[i], BF16)
            a = i // 2
            if i % 2 == 0:
                h = _nsa_mixer(hn, h, bias_near, pat_cmp, nsa_w_in[a], nsa_cmp_pos_k[a], nsa_cmp_pos_v[a],
                               nsa_cmp_w1_k[a], nsa_cmp_w2_k[a], nsa_cmp_w1_v[a], nsa_cmp_w2_v[a],
                               nsa_w_out[a])
            else:
                h = _dsa_mixer(hn, h, bias_near, dsa_w_in[a], dsa_norm_q[a], dsa_norm_kv[a], dsa_w_uq[a],
                               dsa_w_uk[a], dsa_w_uv[a], dsa_w_qidx[a], dsa_idx_ln_g[a],
                               dsa_idx_ln_b[a], dsa_w_out[a])
            hn = _rownorm(h, norm_ffn[i], BF16)
            h = _ffn(hn, h, ffn_w_gate[i], ffn_w_up[i], ffn_w_down[i])
        outs.append(_rownorm(h, norm_final, F32))
    return jnp.stack(outs)
```

```python
import functools
import math

import numpy as np
import jax
import jax.numpy as jnp
from jax import lax
from jax.experimental import pallas as pl
from jax.experimental.pallas import tpu as pltpu

N_HEADS = 32
HEAD_DIM = 128
NUM_BUCKETS = 32
MAX_DISTANCE = 128
Q_BLOCK = 128
NSA_KV_GROUPS = 4
NSA_GROUP_HEADS = N_HEADS // NSA_KV_GROUPS
CMP_LEN = 32
CMP_STRIDE = 16
SEL_BLOCK = 64
SEL_COUNT = 16
WINDOW = 512
FORCE_SCORE = 1e4
Q_RANK = 1024
KV_RANK = 512
IDX_HEADS = 32
IDX_DIM = 128
IDX_TOPK = 256
RMS_EPS = 1e-6

LANES = 128
V7X_VMEM_LIMIT_BYTES = 56 * 1024 * 1024
KEY_TILE = 512
CHUNK = 128

F32 = jnp.float32
BF16 = jnp.bfloat16
NEG_INF = float("-inf")
M_INIT = -1e30
MASK_BIG = 2.0 ** 100
LOG2E = math.log2(math.e)
INT_MIN = -(2 ** 31)


def _cparams(sem):
    return pltpu.CompilerParams(dimension_semantics=sem,
                                vmem_limit_bytes=V7X_VMEM_LIMIT_BYTES)


def _pick(n, candidates):
    for c in candidates:
        if n % c == 0:
            return c
    return n


def _round_up(n, m):
    return (n + m - 1) // m * m


def _rmsnorm_kernel(x_ref, g_ref, o_ref):
    x = x_ref[...].astype(F32)
    ms = jnp.mean(x * x, axis=-1, keepdims=True)
    o_ref[...] = (x * lax.rsqrt(ms + RMS_EPS) * g_ref[...]).astype(o_ref.dtype)


def _layernorm_kernel(x_ref, g_ref, b_ref, o_ref):
    x = x_ref[...].astype(F32)
    mu = jnp.mean(x, axis=-1, keepdims=True)
    xc = x - mu
    var = jnp.mean(xc * xc, axis=-1, keepdims=True)
    o_ref[...] = (xc * lax.rsqrt(var + RMS_EPS) * g_ref[...] + b_ref[...]).astype(o_ref.dtype)


def _rownorm(x, gain, out_dtype, *, width=None, col_block=0, bias=None):
    m = x.shape[0]
    width = x.shape[1] if width is None else width
    tm = _pick(m, (256, 128, 64, 32, 16, 8))
    row = lambda i: (i, col_block)
    vec = lambda i: (0, 0)
    g2 = gain.reshape(1, width).astype(F32)
    if bias is None:
        kern, extra, extra_specs = _rmsnorm_kernel, (), ()
    else:
        kern, extra = _layernorm_kernel, (bias.reshape(1, width).astype(F32),)
        extra_specs = (pl.BlockSpec((1, width), vec),)
    return pl.pallas_call(
        kern,
        grid=(m // tm,),
        in_specs=[pl.BlockSpec((tm, width), row), pl.BlockSpec((1, width), vec), *extra_specs],
        out_specs=pl.BlockSpec((tm, width), lambda i: (i, 0)),
        out_shape=jax.ShapeDtypeStruct((m, width), out_dtype),
        name="rownorm",
        compiler_params=_cparams(("parallel",)),
    )(x, g2, *extra)


def _mm_kernel(*refs, nk, has_res):
    a_ref, b_ref = refs[:2]
    r_ref = refs[2] if has_res else None
    o_ref = refs[3] if has_res else refs[2]
    acc_ref = refs[-1] if nk > 1 else None
    k = pl.program_id(2)
    part = jnp.dot(a_ref[...], b_ref[...], preferred_element_type=F32)

    def finish(acc):
        if has_res:
            acc = acc + r_ref[...]
        o_ref[...] = acc.astype(o_ref.dtype)

    if nk == 1:
        finish(part)
        return

    @pl.when(k == 0)
    def _():
        acc_ref[...] = part

    @pl.when((k > 0) & (k < nk - 1))
    def _():
        acc_ref[...] += part

    @pl.when(k == nk - 1)
    def _():
        finish(acc_ref[...] + part)


MAX_FULL_K = 4096


def _matmul(a, b, out_dtype, res=None):
    m, kd = a.shape
    n = b.shape[1]
    tm = _pick(m, (1024, 512, 256, 128))
    if kd <= MAX_FULL_K:
        tk = kd
        tn_cap = min(2048, max(512, (2 * 1024 * 1024) // kd))
        tn = _pick(n, [c for c in (2048, 1024, 896, 512, 384, 256, 128) if c <= max(tn_cap, 896)])
    else:
        tk = _pick(kd, (2816, 2048, 1024, 512, 256, 128))
        tn = _pick(n, (1024, 512, 256, 128))
    nk = kd // tk
    in_specs = [pl.BlockSpec((tm, tk), lambda i, j, k: (i, k)),
                pl.BlockSpec((tk, tn), lambda i, j, k: (k, j))]
    args = [a, b]
    if res is not None:
        in_specs.append(pl.BlockSpec((tm, tn), lambda i, j, k: (i, j)))
        args.append(res)
    return pl.pallas_call(
        functools.partial(_mm_kernel, nk=nk, has_res=res is not None),
        grid=(m // tm, n // tn, nk),
        in_specs=in_specs,
        out_specs=pl.BlockSpec((tm, tn), lambda i, j, k: (i, j)),
        out_shape=jax.ShapeDtypeStruct((m, n), out_dtype),
        scratch_shapes=[pltpu.VMEM((tm, tn), F32)] if nk > 1 else [],
        name="matmul",
        compiler_params=_cparams(("parallel", "parallel", "arbitrary")),
    )(*args)


def _swiglu_kernel(x_ref, wg_ref, wu_ref, o_ref, *, n_real):
    j = pl.program_id(1)

    @pl.when(j < n_real)
    def _():
        x = x_ref[...]
        g = jnp.dot(x, wg_ref[...].astype(BF16), preferred_element_type=F32)
        u = jnp.dot(x, wu_ref[...].astype(BF16), preferred_element_type=F32)
        o_ref[...] = (g * jax.nn.sigmoid(g) * u).astype(o_ref.dtype)

    @pl.when(j >= n_real)
    def _():
        o_ref[...] = jnp.zeros_like(o_ref)


SWIGLU_TN = 256


def _swiglu_up(x, wg, wu, layer, n_out):
    m, kd = x.shape
    n = wg.shape[2]
    tn = _pick(n, (SWIGLU_TN, 128))
    assert kd <= MAX_FULL_K and n % tn == 0 and n_out % tn == 0
    n_real = n // tn
    tm = _pick(m, (2048, 1024, 512, 256, 128))
    w_spec = pl.BlockSpec((None, kd, tn), lambda i, j: (layer, 0, jnp.minimum(j, n_real - 1)))
    return pl.pallas_call(
        functools.partial(_swiglu_kernel, n_real=n_real),
        grid=(m // tm, n_out // tn),
        in_specs=[pl.BlockSpec((tm, kd), lambda i, j: (i, 0), pipeline_mode=pl.Buffered(1)),
                  w_spec, w_spec],
        out_specs=pl.BlockSpec((tm, tn), lambda i, j: (i, j)),
        out_shape=jax.ShapeDtypeStruct((m, n_out), BF16),
        name="swiglu_up",
        compiler_params=_cparams(("parallel", "arbitrary")),
    )(x, wg, wu)


def _dot_nt(a, b):
    return lax.dot_general(a, b, (((1,), (1,)), ((), ())), preferred_element_type=F32)


def _with_ones(v):
    return jnp.concatenate([v, jnp.ones_like(v)], axis=1)


def _lane_blocks(x):
    return [x[:, c * LANES:(c + 1) * LANES] for c in range(x.shape[1] // LANES)]


def _softmax_step(q, k, v_ones, add, m_ref, acc_ref, h):
    _softmax_update(_dot_nt(q, k) + add, v_ones, m_ref, acc_ref, h)


def _softmax_update(z, v_ones, m_ref, acc_ref, h):
    m_old = m_ref[h]
    m_new = jnp.maximum(m_old, jnp.max(z, axis=1, keepdims=True))
    alpha = jnp.exp2(m_old - m_new)
    p = jnp.concatenate([jnp.exp2(zc - m_new) for zc in _lane_blocks(z)], axis=1).astype(BF16)
    pv = jnp.dot(p, v_ones, preferred_element_type=F32)
    acc_ref[h] = jnp.concatenate([alpha, alpha], axis=1) * acc_ref[h] + pv
    m_ref[h] = m_new


def _softmax_init(m_ref, acc_ref):
    m_ref[...] = jnp.full_like(m_ref, M_INIT)
    acc_ref[...] = jnp.zeros_like(acc_ref)


def _normalized(acc):
    return acc[:, :HEAD_DIM] * (1.0 / jnp.maximum(acc[:, HEAD_DIM:], 1e-30))


def _attend_once(q, k, v_ones, add):
    z = _dot_nt(q, k) + add
    m = jnp.maximum(jnp.max(z, axis=1, keepdims=True), M_INIT)
    p = jnp.exp2(z - m).astype(BF16)
    return _normalized(jnp.dot(p, v_ones, preferred_element_type=F32))


def _t5_bucket(dist):
    n = jnp.maximum(dist, 0)
    max_exact = NUM_BUCKETS // 2
    nf = jnp.maximum(n, 1).astype(F32)
    large = max_exact + (jnp.log(nf / max_exact) / math.log(MAX_DISTANCE / max_exact)
                         * (NUM_BUCKETS - max_exact)).astype(jnp.int32)
    large = jnp.minimum(large, NUM_BUCKETS - 1)
    return jnp.where(n < max_exact, n, large)


def _bias_tiles(rel_bias):
    tab = (rel_bias.astype(F32).T - rel_bias.astype(F32)[NUM_BUCKETS - 1][:, None]) * LOG2E

    def lookup(bucket):
        out = jnp.zeros((tab.shape[0],) + bucket.shape, F32)
        for b in range(NUM_BUCKETS):
            out = jnp.where(bucket[None] == b, tab[:, b][:, None, None], out)
        return out

    i = np.arange(Q_BLOCK)[:, None]
    j = np.arange(2 * CHUNK)[None, :]
    dist_near = jnp.asarray(np.where(j < CHUNK, CHUNK + i - j, i - (j - CHUNK)), jnp.int32)
    near = jnp.where(dist_near >= 0, lookup(_t5_bucket(dist_near)), NEG_INF)
    lane = np.arange(LANES)[None, :]
    dist_cmp = jnp.asarray(i - CMP_STRIDE * (lane - LANES // 2) - (CMP_LEN - 1), jnp.int32)
    cmp_ = jnp.where(dist_cmp >= 0, lookup(_t5_bucket(dist_cmp)), 0.0)
    return near, cmp_


def _compress_kernel(x_ref, pos_ref, w1_ref, w2_ref, o_ref, *, half):
    x = x_ref[...].astype(F32)
    xa = (x + pos_ref[0:1, :]).astype(BF16)
    xb = (x + pos_ref[1:2, :]).astype(BF16)
    a = jnp.dot(xa, w1_ref[0:half, :], preferred_element_type=F32)
    b = jnp.dot(xb, w1_ref[half:2 * half, :], preferred_element_type=F32)
    nc = a.shape[0]
    pre = a + pltpu.roll(b, nc - 1, 0)
    hmid = pre * jax.nn.sigmoid(pre)
    o_ref[...] = jnp.dot(hmid.astype(BF16), w2_ref[...], preferred_element_type=F32).astype(o_ref.dtype)


def _compress(kv_rows, pos, w1, w2):
    two, g, nc, wd = kv_rows.shape
    d = w2.shape[-1]
    return pl.pallas_call(
        functools.partial(_compress_kernel, half=wd),
        grid=(two, g),
        in_specs=[pl.BlockSpec((None, None, nc, wd), lambda a, b: (a, b, 0, 0)),
                  pl.BlockSpec((None, 2, wd), lambda a, b: (a, 0, 0)),
                  pl.BlockSpec((None, 2 * wd, d), lambda a, b: (a, 0, 0)),
                  pl.BlockSpec((None, d, d), lambda a, b: (a, 0, 0))],
        out_specs=pl.BlockSpec((None, None, nc, d), lambda a, b: (a, b, 0, 0)),
        out_shape=jax.ShapeDtypeStruct((two, g, nc, d), BF16),
        name="nsa_compress",
        compiler_params=_cparams(("parallel", "parallel")),
    )(kv_rows, pos, w1, w2)


def _nsa_cmp_kernel(q_ref, kc_ref, vc_ref, ov_ref, pat_ref, o_ref, imp_ref):
    qi = pl.program_id(1)
    q0 = qi * Q_BLOCK
    ncp = kc_ref.shape[0]
    nsel = ov_ref.shape[1]
    rows = lax.broadcasted_iota(jnp.int32, (Q_BLOCK, ncp), 0)
    cols = lax.broadcasted_iota(jnp.int32, (Q_BLOCK, ncp), 1)
    add_mask = jnp.where(cols * CMP_STRIDE + (CMP_LEN - 1) <= q0 + rows, 0.0, NEG_INF)
    shift = (qi * (Q_BLOCK // CMP_STRIDE) + (ncp - LANES // 2)) % ncp
    kc = kc_ref[...]
    rhs = jnp.concatenate([vc_ref[...], ov_ref[...], jnp.ones((ncp, LANES), BF16)], axis=1)
    imp = jnp.zeros((Q_BLOCK, nsel), F32)
    for r in range(NSA_GROUP_HEADS):
        pat = pat_ref[r]
        if ncp > LANES:
            pat = jnp.concatenate([pat, jnp.zeros((Q_BLOCK, ncp - LANES), F32)], axis=1)
        bias = pltpu.roll(pat, shift, 1)
        z = _dot_nt(q_ref[:, r * HEAD_DIM:(r + 1) * HEAD_DIM], kc) + bias + add_mask
        m = jnp.maximum(jnp.max(z, axis=1, keepdims=True), M_INIT)
        e = jnp.exp2(z - m).astype(BF16)
        acc = jnp.dot(e, rhs, preferred_element_type=F32)
        inv = 1.0 / jnp.maximum(acc[:, HEAD_DIM + nsel:], 1e-30)
        o_ref[:, r * HEAD_DIM:(r + 1) * HEAD_DIM] = (acc[:, :HEAD_DIM] * inv).astype(o_ref.dtype)
        imp = imp + acc[:, HEAD_DIM:HEAD_DIM + nsel] * inv[:, :nsel]
    imp_ref[...] = imp


def _nsa_select_kernel(imp_ref, msel_ref, *, n_top):
    qi = pl.program_id(0)
    ng, _, nsel = imp_ref.shape
    t = qi * Q_BLOCK + lax.broadcasted_iota(jnp.int32, (Q_BLOCK, nsel), 0)
    j = lax.broadcasted_iota(jnp.int32, (Q_BLOCK, nsel), 1)
    blk_t = t // SEL_BLOCK
    forced = (j == 0) | (j == blk_t) | (j == blk_t - 1)
    visible = j * SEL_BLOCK <= t
    jf = j.astype(F32)
    imps = tuple(jnp.where(visible, jnp.where(forced, FORCE_SCORE, imp_ref[g]), -1.0) for g in range(ng))
    sels = tuple(jnp.zeros((Q_BLOCK, nsel), F32) for _ in range(ng))

    def pick(_, carry):
        out_i, out_s = [], []
        for imp_c, sel_c in zip(*carry):
            best = jnp.max(imp_c, axis=1, keepdims=True)
            first = jnp.min(jnp.where(imp_c == best, jf, float(nsel)), axis=1, keepdims=True)
            hit = jf == first
            out_i.append(jnp.where(hit, NEG_INF, imp_c))
            out_s.append(jnp.where(hit, 1.0, sel_c))
        return tuple(out_i), tuple(out_s)

    _, sels = lax.fori_loop(0, n_top, pick, (imps, sels))
    for g in range(ng):
        msel_ref[g] = jnp.where(sels[g] > 0.5, 0.0, -MASK_BIG).astype(msel_ref.dtype)


def _nsa_select(imp):
    g, s, nsel = imp.shape
    blk = pl.BlockSpec((g, Q_BLOCK, nsel), lambda i: (0, i, 0))
    return pl.pallas_call(
        functools.partial(_nsa_select_kernel, n_top=min(SEL_COUNT, nsel)),
        grid=(s // Q_BLOCK,),
        in_specs=[blk],
        out_specs=blk,
        out_shape=jax.ShapeDtypeStruct((g, s, nsel), BF16),
        name="nsa_select",
        compiler_params=_cparams(("parallel",)),
    )(imp)


def _nsa_compressed(q, kv_cmp, overlap, pat_cmp):
    s = q.shape[0]
    g, ncp, d = kv_cmp.shape[1:]
    nsel = overlap.shape[1]
    gw = NSA_GROUP_HEADS * HEAD_DIM
    return pl.pallas_call(
        _nsa_cmp_kernel,
        grid=(g, s // Q_BLOCK),
        in_specs=[pl.BlockSpec((Q_BLOCK, gw), lambda a, b: (b, a)),
                  pl.BlockSpec((None, None, ncp, d), lambda a, b: (0, a, 0, 0)),
                  pl.BlockSpec((None, None, ncp, d), lambda a, b: (1, a, 0, 0)),
                  pl.BlockSpec((ncp, nsel), lambda a, b: (0, 0)),
                  pl.BlockSpec((NSA_GROUP_HEADS, Q_BLOCK, LANES), lambda a, b: (a, 0, 0))],
        out_specs=[pl.BlockSpec((Q_BLOCK, gw), lambda a, b: (b, a)),
                   pl.BlockSpec((None, Q_BLOCK, nsel), lambda a, b: (a, b, 0))],
        out_shape=[jax.ShapeDtypeStruct((s, N_HEADS * HEAD_DIM), BF16),
                   jax.ShapeDtypeStruct((g, s, nsel), F32)],
        name="nsa_cmp_attn",
        compiler_params=_cparams(("parallel", "parallel")),
    )(q, kv_cmp, kv_cmp, overlap, pat_cmp)


def _masked_attn_kernel(*refs, nh, kv_shared, mode):
    if mode == "sel":
        q_ref, k_ref, v_ref, bias_ref, msel_ref, et_ref, o_ref, m_ref, acc_ref, z0_ref, z1_ref = refs
    else:
        q_ref, k_ref, v_ref, bias_ref, mask_ref, o_ref, m_ref, acc_ref, z0_ref, z1_ref = refs
    qi = pl.program_id(1)
    q0 = qi * Q_BLOCK
    far_end = q0 - CHUNK
    _softmax_init(m_ref, acc_ref)
    tile_chunks = KEY_TILE // CHUNK

    def head_q(h):
        return q_ref[:, h * HEAD_DIM:(h + 1) * HEAD_DIM]

    def head_k(rows, h):
        return k_ref[rows, :] if kv_shared else k_ref[rows, h * HEAD_DIM:(h + 1) * HEAD_DIM]

    def head_v(rows, h):
        return v_ref[rows, :] if kv_shared else v_ref[rows, h * HEAD_DIM:(h + 1) * HEAD_DIM]

    def mask_chunks(c0, n):
        if mode == "sel":
            return _dot_nt(msel_ref[...], et_ref[pl.ds(pl.multiple_of(c0 * CHUNK, CHUNK), n * CHUNK), :])
        parts = [mask_ref[c0 + c].astype(F32) for c in range(n)]
        return parts[0] if n == 1 else jnp.concatenate(parts, axis=1)

    n_far = lax.shift_right_logical(jnp.maximum(far_end, 0) + KEY_TILE - 1, KEY_TILE.bit_length() - 1)

    def tile_rows(kt):
        return pl.ds(pl.multiple_of(jnp.minimum(kt, n_far - 1) * KEY_TILE, KEY_TILE), KEY_TILE)

    def logits(kt, zbuf):
        cols = kt * KEY_TILE + lax.broadcasted_iota(jnp.int32, (Q_BLOCK, KEY_TILE), 1)
        chunk0 = jnp.minimum(kt, n_far - 1) * tile_chunks
        add = jnp.where(cols < far_end, mask_chunks(chunk0, tile_chunks), NEG_INF)
        rows = tile_rows(kt)
        for h in range(nh):
            zbuf[h] = _dot_nt(head_q(h), head_k(rows, h)) + add

    def consume(kt, zbuf):
        rows = tile_rows(kt)
        for h in range(nh):
            _softmax_update(zbuf[h], _with_ones(head_v(rows, h)), m_ref, acc_ref, h)

    @pl.when(n_far > 0)
    def _():
        logits(0, z0_ref)

    def far_pair(pp, carry):
        kt = 2 * pp
        logits(kt + 1, z1_ref)
        consume(kt, z0_ref)
        logits(kt + 2, z0_ref)
        consume(kt + 1, z1_ref)
        return carry

    lax.fori_loop(0, lax.shift_right_logical(n_far, 1), far_pair, 0)

    @pl.when((n_far & 1) == 1)
    def _():
        consume(n_far - 1, z0_ref)

    prev0 = pl.multiple_of(jnp.maximum(far_end, 0), CHUNK)
    diag0 = pl.multiple_of(q0, CHUNK)
    no_prev = jnp.where(qi > 0, 0.0, NEG_INF)
    add_prev = mask_chunks(jnp.maximum(qi - 1, 0), 1) + no_prev
    add_diag = mask_chunks(qi, 1)
    add_near = jnp.concatenate([add_prev, add_diag], axis=1)

    def near_kv(h):
        k = jnp.concatenate([head_k(pl.ds(prev0, CHUNK), h), head_k(pl.ds(diag0, CHUNK), h)], axis=0)
        v = jnp.concatenate([head_v(pl.ds(prev0, CHUNK), h), head_v(pl.ds(diag0, CHUNK), h)], axis=0)
        return k, _with_ones(v)

    for h in range(nh):
        k, v1 = near_kv(h)
        _softmax_step(head_q(h), k, v1, add_near + bias_ref[h], m_ref, acc_ref, h)
    for h in range(nh):
        o_ref[:, h * HEAD_DIM:(h + 1) * HEAD_DIM] = _normalized(acc_ref[h]).astype(o_ref.dtype)


def _masked_attention(q, k, v, bias_near, *, mode, nh, k_col0=0, v_col0=0, msel=None, expand_t=None,
                      mask=None):
    s = q.shape[0]
    n_groups = N_HEADS // nh
    qw = nh * HEAD_DIM
    kv_shared = mode == "sel"
    q_spec = pl.BlockSpec((Q_BLOCK, qw), lambda a, b: (b, a))
    bias_spec = pl.BlockSpec((nh, Q_BLOCK, 2 * CHUNK), lambda a, b: (a, 0, 0))
    if mode == "sel":
        nsel = msel.shape[-1]
        in_specs = [q_spec,
                    pl.BlockSpec((s, HEAD_DIM), lambda a, b: (0, k_col0 + a)),
                    pl.BlockSpec((s, HEAD_DIM), lambda a, b: (0, v_col0 + a)),
                    bias_spec,
                    pl.BlockSpec((None, Q_BLOCK, nsel), lambda a, b: (a, b, 0)),
                    pl.BlockSpec((s, nsel), lambda a, b: (0, 0))]
        args = [q, k, v, bias_near, msel, expand_t]
    else:
        nch = s // CHUNK
        once = pl.Buffered(1)
        in_specs = [q_spec,
                    pl.BlockSpec((s, qw), lambda a, b: (0, a), pipeline_mode=once),
                    pl.BlockSpec((s, qw), lambda a, b: (0, a), pipeline_mode=once),
                    bias_spec,
                    pl.BlockSpec((None, nch, Q_BLOCK, CHUNK), lambda a, b: (b, 0, 0, 0))]
        args = [q, k, v, bias_near, mask]
    return pl.pallas_call(
        functools.partial(_masked_attn_kernel, nh=nh, kv_shared=kv_shared, mode=mode),
        grid=(n_groups, s // Q_BLOCK),
        in_specs=in_specs,
        out_specs=pl.BlockSpec((Q_BLOCK, qw), lambda a, b: (b, a)),
        out_shape=jax.ShapeDtypeStruct((s, N_HEADS * HEAD_DIM), BF16),
        scratch_shapes=[pltpu.VMEM((nh, Q_BLOCK, LANES), F32),
                        pltpu.VMEM((nh, Q_BLOCK, 2 * HEAD_DIM), F32),
                        pltpu.VMEM((nh, Q_BLOCK, KEY_TILE), F32),
                        pltpu.VMEM((nh, Q_BLOCK, KEY_TILE), F32)],
        name="masked_attn_" + mode,
        compiler_params=_cparams(("parallel", "arbitrary")),
    )(*args)


def _nsa_win_kernel(q_ref, k_ref, v_ref, bias_ref, gl_ref, oc_ref, os_ref, o_ref, *, n_far):
    g = pl.program_id(0)
    qi = pl.program_id(1)
    q0 = qi * Q_BLOCK
    rows = lax.broadcasted_iota(jnp.int32, (Q_BLOCK, CHUNK), 0)
    cols = lax.broadcasted_iota(jnp.int32, (Q_BLOCK, CHUNK), 1)
    adds, ks, vs = [], [], []
    for c in range(n_far + 2):
        back = n_far + 1 - c
        start = pl.multiple_of(jnp.maximum(q0 - back * CHUNK, 0), CHUNK)
        valid = qi >= back
        if c == 0:
            ok = valid & (cols > rows)
        else:
            ok = jnp.broadcast_to(valid, (Q_BLOCK, CHUNK))
        adds.append(jnp.where(ok, 0.0, NEG_INF))
        ks.append(k_ref[pl.ds(start, CHUNK), :])
        vs.append(v_ref[pl.ds(start, CHUNK), :])
    k_all = jnp.concatenate(ks, axis=0)
    v_all = _with_ones(jnp.concatenate(vs, axis=0))
    add_far = jnp.concatenate(adds[:n_far], axis=1)
    add_near = jnp.concatenate(adds[n_far:], axis=1)
    gates = jax.nn.sigmoid(gl_ref[...])
    lane = lax.broadcasted_iota(jnp.int32, (Q_BLOCK, LANES), 1)
    for h in range(NSA_GROUP_HEADS):
        sl = slice(h * HEAD_DIM, (h + 1) * HEAD_DIM)
        head = g * NSA_GROUP_HEADS + h
        gate = [jnp.sum(jnp.where(lane == br * N_HEADS + head, gates, 0.0), axis=1, keepdims=True)
                for br in range(3)]
        add = jnp.concatenate([add_far, add_near + bias_ref[h]], axis=1)
        o_w = _attend_once(q_ref[:, sl], k_all, v_all, add)
        o = (gate[0] * oc_ref[:, sl].astype(F32) + gate[1] * os_ref[:, sl].astype(F32)
             + gate[2] * o_w)
        o_ref[:, sl] = o.astype(o_ref.dtype)


def _nsa_window_combine(q, kv, k_col0, v_col0, bias_near, gate_logits, o_c, o_s):
    s = q.shape[0]
    g = NSA_KV_GROUPS
    gw = NSA_GROUP_HEADS * HEAD_DIM
    n_far = WINDOW // CHUNK - 1
    blk = pl.BlockSpec((Q_BLOCK, gw), lambda a, b: (b, a))
    return pl.pallas_call(
        functools.partial(_nsa_win_kernel, n_far=n_far),
        grid=(g, s // Q_BLOCK),
        in_specs=[blk,
                  pl.BlockSpec((s, HEAD_DIM), lambda a, b: (0, k_col0 + a)),
                  pl.BlockSpec((s, HEAD_DIM), lambda a, b: (0, v_col0 + a)),
                  pl.BlockSpec((NSA_GROUP_HEADS, Q_BLOCK, 2 * CHUNK), lambda a, b: (a, 0, 0)),
                  pl.BlockSpec((Q_BLOCK, LANES), lambda a, b: (b, 0)),
                  blk, blk],
        out_specs=blk,
        out_shape=jax.ShapeDtypeStruct((s, N_HEADS * HEAD_DIM), BF16),
        name="nsa_window",
        compiler_params=_cparams(("parallel", "parallel")),
    )(q, kv, kv, bias_near, gate_logits, o_c, o_s)


def _dsa_index_kernel(qx_ref, w_ref, kx_ref, mask_ref, key_ref, wb_ref, jp_ref, *, top_k, w_scale):
    qi = pl.program_id(0)
    q0 = qi * Q_BLOCK
    n_ch = qi + 1
    nch_total = key_ref.shape[0]
    s_total = nch_total * CHUNK
    kf = float(top_k)
    w = w_ref[...] * w_scale
    lane = lax.broadcasted_iota(jnp.int32, (Q_BLOCK, LANES), 1)
    for h in range(IDX_HEADS):
        col = jnp.sum(jnp.where(lane == h, w, 0.0), axis=1, keepdims=True)
        wb_ref[h] = jnp.broadcast_to(col, (Q_BLOCK, LANES))
    rows = lax.broadcasted_iota(jnp.int32, (Q_BLOCK, CHUNK), 0)
    cols = lax.broadcasted_iota(jnp.int32, (Q_BLOCK, CHUNK), 1)
    rows2 = lax.broadcasted_iota(jnp.int32, (Q_BLOCK, 2 * CHUNK), 0)
    cols2 = lax.broadcasted_iota(jnp.int32, (Q_BLOCK, 2 * CHUNK), 1)

    def score_pair(pp, carry):
        start = pl.multiple_of(pp * (2 * CHUNK), 2 * CHUNK)
        kx = kx_ref[pl.ds(start, 2 * CHUNK), :]
        acc = jnp.zeros((Q_BLOCK, 2 * CHUNK), F32)
        for h in range(IDX_HEADS):
            sc = _dot_nt(qx_ref[:, h * IDX_DIM:(h + 1) * IDX_DIM], kx)
            wb = wb_ref[h]
            acc = acc + jnp.maximum(sc, 0.0) * jnp.concatenate([wb, wb], axis=1)
        acc = jnp.where(start + cols2 <= q0 + rows2, acc, NEG_INF)
        bits = lax.bitcast_convert_type(acc, jnp.int32)
        keys = jnp.where(bits >= 0, bits, bits ^ jnp.int32(0x7FFFFFFF))
        key_ref[2 * pp] = keys[:, :CHUNK]
        key_ref[2 * pp + 1] = keys[:, CHUNK:]
        return carry

    n_pair = lax.shift_right_logical(n_ch + 1, 1)
    lax.fori_loop(0, n_pair, score_pair, 0)

    def count(pred):
        def body(pp, cnt):
            for c in range(2):
                cnt = cnt + jnp.where(pred(2 * pp + c, key_ref[2 * pp + c]), 1.0, 0.0)
            return cnt
        cnt = lax.fori_loop(0, n_pair, body, jnp.zeros((Q_BLOCK, CHUNK), F32))
        return jnp.sum(cnt, axis=1, keepdims=True)

    def unsettled(st):
        b, _, cnt_thr = st
        return (b < 32) & (jnp.max(jnp.abs(cnt_thr - kf)) > 0.5)

    def bisect(st):
        b, thr, cnt_thr = st
        cand = thr + lax.shift_left(jnp.int32(1), jnp.int32(31) - b)
        cand_b = jnp.broadcast_to(cand, (Q_BLOCK, CHUNK))
        cnt = count(lambda ch, key: key >= cand_b)
        ge = cnt >= kf
        return b + 1, jnp.where(ge, cand, thr), jnp.where(ge, cnt, cnt_thr)

    total = (n_pair * (2 * CHUNK)).astype(F32)
    _, thr, cnt_thr = lax.while_loop(
        unsettled, bisect,
        (jnp.int32(0), jnp.full((Q_BLOCK, 1), INT_MIN, jnp.int32), jnp.full((Q_BLOCK, 1), total, F32)))
    thr_b = jnp.broadcast_to(thr, (Q_BLOCK, CHUNK))

    tie = cnt_thr > kf
    jp_ref[...] = jnp.full((Q_BLOCK, CHUNK), s_total, jnp.int32)

    @pl.when(jnp.max(jnp.where(tie, 1.0, 0.0)) > 0.5)
    def _():
        need = kf - count(lambda ch, key: key > thr_b)

        def refine(i, jp):
            cand = jp + lax.shift_left(jnp.int32(1), jnp.int32(s_total.bit_length() - 2) - i)
            cand_b = jnp.broadcast_to(cand, (Q_BLOCK, CHUNK))
            below = count(lambda ch, key: (key == thr_b) & (ch * CHUNK + cols < cand_b))
            return jnp.where(below < need, cand, jp)

        jp = lax.fori_loop(0, s_total.bit_length() - 1, refine, jnp.zeros((Q_BLOCK, 1), jnp.int32))
        jp_ref[...] = jnp.broadcast_to(jnp.where(tie, jp, s_total), (Q_BLOCK, CHUNK))

    jp_b = jp_ref[...]

    def write_chunk(ch, carry):
        key = key_ref[ch]
        pos = ch * CHUNK + cols
        keep = ((key > thr_b) | ((key == thr_b) & (pos <= jp_b))) & (pos <= q0 + rows)
        mask_ref[ch] = jnp.where(keep, 0.0, NEG_INF).astype(mask_ref.dtype)
        return carry

    lax.fori_loop(0, n_ch, write_chunk, 0)

    def blank_chunk(ch, carry):
        mask_ref[ch] = jnp.full((Q_BLOCK, CHUNK), NEG_INF, mask_ref.dtype)
        return carry

    lax.fori_loop(n_ch, nch_total, blank_chunk, 0)


def _dsa_index_mask(q_idx, w_idx_src, w_col_block, k_idx, top_k):
    s = q_idx.shape[0]
    nq, nch = s // Q_BLOCK, s // CHUNK
    w_scale = IDX_HEADS ** -0.5 * IDX_DIM ** -0.5
    return pl.pallas_call(
        functools.partial(_dsa_index_kernel, top_k=top_k, w_scale=w_scale),
        grid=(nq,),
        in_specs=[pl.BlockSpec((Q_BLOCK, IDX_HEADS * IDX_DIM), lambda i: (i, 0)),
                  pl.BlockSpec((Q_BLOCK, LANES), lambda i: (i, w_col_block)),
                  pl.BlockSpec((s, IDX_DIM), lambda i: (0, 0))],
        out_specs=pl.BlockSpec((None, nch, Q_BLOCK, CHUNK), lambda i: (i, 0, 0, 0)),
        out_shape=jax.ShapeDtypeStruct((nq, nch, Q_BLOCK, CHUNK), BF16),
        scratch_shapes=[pltpu.VMEM((nch, Q_BLOCK, CHUNK), jnp.int32),
                        pltpu.VMEM((IDX_HEADS, Q_BLOCK, LANES), F32),
                        pltpu.VMEM((Q_BLOCK, CHUNK), jnp.int32)],
        name="dsa_index",
        compiler_params=_cparams(("parallel",)),
    )(q_idx, w_idx_src, k_idx)


def _nsa_constants(s):
    ncp = s // CMP_STRIDE
    nsel = s // SEL_BLOCK
    c0 = np.arange(ncp) * CMP_STRIDE
    sel0 = np.arange(nsel) * SEL_BLOCK
    overlap = ((c0[:, None] < sel0[None, :] + SEL_BLOCK)
               & (c0[:, None] + CMP_LEN - 1 >= sel0[None, :])).astype(np.float32)
    expand_t = (np.arange(s)[:, None] // SEL_BLOCK == np.arange(nsel)[None, :]).astype(np.float32)
    return jnp.asarray(overlap, BF16), jnp.asarray(expand_t, BF16)


def _nsa_mixer(hn, h_res, bias_near, pat_cmp, w_in, pos_k, pos_v, w1_k, w2_k, w1_v, w2_v, w_out):
    s = hn.shape[0]
    g, d = NSA_KV_GROUPS, HEAD_DIM
    hd = N_HEADS * d
    q_scale = d ** -0.5 * LOG2E
    w_q = (w_in[:, :hd] * q_scale).astype(BF16)
    w_kv = w_in[:, hd:hd + 6 * g * d].astype(BF16)
    w_g = jnp.pad(w_in[:, hd + 6 * g * d:], ((0, 0), (0, LANES - 3 * N_HEADS))).astype(BF16)
    q = _matmul(hn, w_q, BF16)
    kv = _matmul(hn, w_kv, BF16)
    gate_logits = _matmul(hn, w_g, F32)
    kv6 = kv.reshape(s, 6, g, d)
    rows16 = kv6[:, :2].transpose(1, 2, 0, 3).reshape(2, g, s // CMP_STRIDE, CMP_STRIDE * d)
    pos = jnp.stack([pos_k.reshape(2, CMP_STRIDE * d), pos_v.reshape(2, CMP_STRIDE * d)]).astype(F32)
    kv_cmp = _compress(rows16, pos,
                       jnp.stack([w1_k, w1_v]).astype(BF16), jnp.stack([w2_k, w2_v]).astype(BF16))
    overlap, expand_t = _nsa_constants(s)
    o_c, imp = _nsa_compressed(q, kv_cmp, overlap, pat_cmp)
    msel = _nsa_select(imp)
    o_s = _masked_attention(q, kv, kv, bias_near, mode="sel", nh=NSA_GROUP_HEADS, k_col0=2 * g, v_col0=3 * g,
                            msel=msel, expand_t=expand_t)
    o = _nsa_window_combine(q, kv, 4 * g, 5 * g, bias_near, gate_logits, o_c, o_s)
    return _matmul(o, w_out.astype(BF16), F32, res=h_res)


def _dsa_mixer(hn, h_res, bias_near, w_in, g_q, g_kv, w_uq, w_uk, w_uv, w_qidx, ln_g, ln_b, w_out):
    s = hn.shape[0]
    d = HEAD_DIM
    n_in = w_in.shape[1]
    w_in_p = jnp.pad(w_in, ((0, 0), (0, _round_up(n_in, LANES) - n_in))).astype(BF16)
    proj = _matmul(hn, w_in_p, F32)
    c_q = _rownorm(proj, g_q, BF16, width=Q_RANK, col_block=0)
    c_kv = _rownorm(proj, g_kv, BF16, width=KV_RANK, col_block=Q_RANK // KV_RANK)
    k_idx = _rownorm(proj, ln_g, BF16, width=IDX_DIM, col_block=(Q_RANK + KV_RANK) // IDX_DIM, bias=ln_b)
    q_scale = d ** -0.5 * LOG2E
    q = _matmul(c_q, (w_uq * q_scale).astype(BF16), BF16)
    q_idx = _matmul(c_q, w_qidx.astype(BF16), BF16)
    w_k = w_uk.transpose(2, 0, 1).reshape(KV_RANK, N_HEADS * d).astype(BF16)
    w_v = w_uv.transpose(1, 0, 2).reshape(KV_RANK, N_HEADS * d).astype(BF16)
    k = _matmul(c_kv, w_k, BF16)
    v = _matmul(c_kv, w_v, BF16)
    top_k = min(IDX_TOPK, s // 4)
    mask = _dsa_index_mask(q_idx, proj, (Q_RANK + KV_RANK + IDX_DIM) // LANES, k_idx, top_k)
    o = _masked_attention(q, k, v, bias_near, mode="dsa", nh=8, mask=mask)
    return _matmul(o, w_out.astype(BF16), F32, res=h_res)


def _ffn(hn, h_res, w_gate_all, w_up_all, w_down, layer):
    f = w_down.shape[0]
    fp = _round_up(f, 1024) if f > 1024 else _round_up(f, LANES)
    wd = jnp.pad(w_down.astype(BF16), ((0, fp - f), (0, 0)))
    return _matmul(_swiglu_up(hn, w_gate_all, w_up_all, layer, fp), wd, F32, res=h_res)


def kernel(x, rel_bias, norm_mix, norm_ffn, norm_final, ffn_w_gate, ffn_w_up, ffn_w_down, nsa_w_in, nsa_cmp_pos_k, nsa_cmp_pos_v, nsa_cmp_w1_k, nsa_cmp_w2_k, nsa_cmp_w1_v, nsa_cmp_w2_v, nsa_w_out, dsa_w_in, dsa_norm_q, dsa_norm_kv, dsa_w_uq, dsa_w_uk, dsa_w_uv, dsa_w_qidx, dsa_idx_ln_g, dsa_idx_ln_b, dsa_w_out):
    b, s, dm = x.shape
    assert s % KEY_TILE == 0 and s // CMP_STRIDE >= LANES
    depth = norm_mix.shape[0]
    bias_near, pat_cmp = _bias_tiles(rel_bias)
    outs = []
    for bi in range(b):
        h = x[bi]
        for i in range(depth):
            hn = _rownorm(h, norm_mix[i], BF16)
            a = i // 2
            if i % 2 == 0:
                h = _nsa_mixer(hn, h, bias_near, pat_cmp, nsa_w_in[a], nsa_cmp_pos_k[a], nsa_cmp_pos_v[a],
                               nsa_cmp_w1_k[a], nsa_cmp_w2_k[a], nsa_cmp_w1_v[a], nsa_cmp_w2_v[a],
                               nsa_w_out[a])
            else:
                h = _dsa_mixer(hn, h, bias_near, dsa_w_in[a], dsa_norm_q[a], dsa_norm_kv[a], dsa_w_uq[a],
                               dsa_w_uk[a], dsa_w_uv[a], dsa_w_qidx[a], dsa_idx_ln_g[a],
                               dsa_idx_ln_b[a], dsa_w_out[a])
            hn = _rownorm(h, norm_ffn[i], BF16)
            h = _ffn(hn, h, ffn_w_gate, ffn_w_up, ffn_w_down[i], i)
        outs.append(_rownorm(h, norm_final, F32))
    return jnp.stack(outs)
```

```python
import functools
import math

import numpy as np
import jax
import jax.numpy as jnp
from jax import lax
from jax.experimental import pallas as pl
from jax.experimental.pallas import tpu as pltpu

N_HEADS = 32
HEAD_DIM = 128
NUM_BUCKETS = 32
MAX_DISTANCE = 128
Q_BLOCK = 128
NSA_KV_GROUPS = 4
NSA_GROUP_HEADS = N_HEADS // NSA_KV_GROUPS
CMP_LEN = 32
CMP_STRIDE = 16
SEL_BLOCK = 64
SEL_COUNT = 16
WINDOW = 512
FORCE_SCORE = 1e4
Q_RANK = 1024
KV_RANK = 512
IDX_HEADS = 32
IDX_DIM = 128
IDX_TOPK = 256
RMS_EPS = 1e-6

LANES = 128
V7X_VMEM_LIMIT_BYTES = 56 * 1024 * 1024
KEY_TILE = 512
CHUNK = 128

F32 = jnp.float32
BF16 = jnp.bfloat16
NEG_INF = float("-inf")
M_INIT = -1e30
MASK_BIG = 2.0 ** 100
LOG2E = math.log2(math.e)
INT_MIN = -(2 ** 31)


def _cparams(sem):
    return pltpu.CompilerParams(dimension_semantics=sem,
                                vmem_limit_bytes=V7X_VMEM_LIMIT_BYTES)


def _pick(n, candidates):
    for c in candidates:
        if n % c == 0:
            return c
    return n


def _round_up(n, m):
    return (n + m - 1) // m * m


def _rmsnorm_kernel(x_ref, g_ref, o_ref):
    x = x_ref[...].astype(F32)
    ms = jnp.mean(x * x, axis=-1, keepdims=True)
    o_ref[...] = (x * lax.rsqrt(ms + RMS_EPS) * g_ref[...]).astype(o_ref.dtype)


def _layernorm_kernel(x_ref, g_ref, b_ref, o_ref):
    x = x_ref[...].astype(F32)
    mu = jnp.mean(x, axis=-1, keepdims=True)
    xc = x - mu
    var = jnp.mean(xc * xc, axis=-1, keepdims=True)
    o_ref[...] = (xc * lax.rsqrt(var + RMS_EPS) * g_ref[...] + b_ref[...]).astype(o_ref.dtype)


def _rownorm(x, gain, out_dtype, *, width=None, col_block=0, bias=None):
    m = x.shape[0]
    width = x.shape[1] if width is None else width
    tm = _pick(m, (256, 128, 64, 32, 16, 8))
    row = lambda i: (i, col_block)
    vec = lambda i: (0, 0)
    g2 = gain.reshape(1, width).astype(F32)
    if bias is None:
        kern, extra, extra_specs = _rmsnorm_kernel, (), ()
    else:
        kern, extra = _layernorm_kernel, (bias.reshape(1, width).astype(F32),)
        extra_specs = (pl.BlockSpec((1, width), vec),)
    return pl.pallas_call(
        kern,
        grid=(m // tm,),
        in_specs=[pl.BlockSpec((tm, width), row), pl.BlockSpec((1, width), vec), *extra_specs],
        out_specs=pl.BlockSpec((tm, width), lambda i: (i, 0)),
        out_shape=jax.ShapeDtypeStruct((m, width), out_dtype),
        name="rownorm",
        compiler_params=_cparams(("parallel",)),
    )(x, g2, *extra)


def _mm_kernel(*refs, nk, has_res):
    a_ref, b_ref = refs[:2]
    r_ref = refs[2] if has_res else None
    o_ref = refs[3] if has_res else refs[2]
    acc_ref = refs[-1] if nk > 1 else None
    k = pl.program_id(2)
    part = jnp.dot(a_ref[...], b_ref[...], preferred_element_type=F32)

    def finish(acc):
        if has_res:
            acc = acc + r_ref[...]
        o_ref[...] = acc.astype(o_ref.dtype)

    if nk == 1:
        finish(part)
        return

    @pl.when(k == 0)
    def _():
        acc_ref[...] = part

    @pl.when((k > 0) & (k < nk - 1))
    def _():
        acc_ref[...] += part

    @pl.when(k == nk - 1)
    def _():
        finish(acc_ref[...] + part)


MAX_FULL_K = 4096


def _matmul(a, b, out_dtype, res=None):
    m, kd = a.shape
    n = b.shape[1]
    tm = _pick(m, (1024, 512, 256, 128))
    if kd <= MAX_FULL_K:
        tk = kd
        tn_cap = min(2048, max(512, (2 * 1024 * 1024) // kd))
        tn = _pick(n, [c for c in (2048, 1024, 896, 512, 384, 256, 128) if c <= max(tn_cap, 896)])
    else:
        tk = _pick(kd, (2816, 2048, 1024, 512, 256, 128))
        tn = _pick(n, (1024, 512, 256, 128))
    nk = kd // tk
    in_specs = [pl.BlockSpec((tm, tk), lambda i, j, k: (i, k)),
                pl.BlockSpec((tk, tn), lambda i, j, k: (k, j))]
    args = [a, b]
    if res is not None:
        in_specs.append(pl.BlockSpec((tm, tn), lambda i, j, k: (i, j)))
        args.append(res)
    return pl.pallas_call(
        functools.partial(_mm_kernel, nk=nk, has_res=res is not None),
        grid=(m // tm, n // tn, nk),
        in_specs=in_specs,
        out_specs=pl.BlockSpec((tm, tn), lambda i, j, k: (i, j)),
        out_shape=jax.ShapeDtypeStruct((m, n), out_dtype),
        scratch_shapes=[pltpu.VMEM((tm, tn), F32)] if nk > 1 else [],
        name="matmul",
        compiler_params=_cparams(("parallel", "parallel", "arbitrary")),
    )(*args)


def _swiglu_kernel(x_ref, wg_ref, wu_ref, o_ref, *, n_real):
    j = pl.program_id(1)

    @pl.when(j < n_real)
    def _():
        x = x_ref[...]
        g = jnp.dot(x, wg_ref[...].astype(BF16), preferred_element_type=F32)
        u = jnp.dot(x, wu_ref[...].astype(BF16), preferred_element_type=F32)
        o_ref[...] = (g * jax.nn.sigmoid(g) * u).astype(o_ref.dtype)

    @pl.when(j >= n_real)
    def _():
        o_ref[...] = jnp.zeros_like(o_ref)


SWIGLU_TN = 256


def _swiglu_up(x, wg, wu, layer, n_out):
    m, kd = x.shape
    n = wg.shape[2]
    tn = _pick(n, (SWIGLU_TN, 128))
    assert kd <= MAX_FULL_K and n % tn == 0 and n_out % tn == 0
    n_real = n // tn
    tm = _pick(m, (2048, 1024, 512, 256, 128))
    w_spec = pl.BlockSpec((None, kd, tn), lambda i, j: (layer, 0, jnp.minimum(j, n_real - 1)))
    return pl.pallas_call(
        functools.partial(_swiglu_kernel, n_real=n_real),
        grid=(m // tm, n_out // tn),
        in_specs=[pl.BlockSpec((tm, kd), lambda i, j: (i, 0), pipeline_mode=pl.Buffered(1)),
                  w_spec, w_spec],
        out_specs=pl.BlockSpec((tm, tn), lambda i, j: (i, j)),
        out_shape=jax.ShapeDtypeStruct((m, n_out), BF16),
        name="swiglu_up",
        compiler_params=_cparams(("parallel", "arbitrary")),
    )(x, wg, wu)


def _dot_nt(a, b):
    return lax.dot_general(a, b, (((1,), (1,)), ((), ())), preferred_element_type=F32)


def _with_ones(v):
    return jnp.concatenate([v, jnp.ones_like(v)], axis=1)


def _lane_blocks(x):
    return [x[:, c * LANES:(c + 1) * LANES] for c in range(x.shape[1] // LANES)]


def _softmax_update(z, v_ones, m_ref, acc_ref, h):
    m_old = m_ref[h]
    m_new = jnp.maximum(m_old, jnp.max(z, axis=1, keepdims=True))
    alpha = jnp.exp2(m_old - m_new)
    p = jnp.concatenate([jnp.exp2(zc - m_new) for zc in _lane_blocks(z)], axis=1).astype(BF16)
    pv = jnp.dot(p, v_ones, preferred_element_type=F32)
    acc_ref[h] = jnp.concatenate([alpha, alpha], axis=1) * acc_ref[h] + pv
    m_ref[h] = m_new


def _softmax_init(m_ref, acc_ref):
    m_ref[...] = jnp.full_like(m_ref, M_INIT)
    acc_ref[...] = jnp.zeros_like(acc_ref)


def _normalized(acc):
    return acc[:, :HEAD_DIM] * (1.0 / jnp.maximum(acc[:, HEAD_DIM:], 1e-30))


def _attend_once(z, v_ones):
    m =jnp.maximum(jnp.max(z, axis=1, keepdims=True), M_INIT)
    p = jnp.exp2(z - m).astype(BF16)
    return _normalized(jnp.dot(p, v_ones, preferred_element_type=F32))


def _t5_bucket(dist):
    n = jnp.maximum(dist, 0)
    max_exact = NUM_BUCKETS // 2
    nf = jnp.maximum(n, 1).astype(F32)
    large = max_exact + (jnp.log(nf / max_exact) / math.log(MAX_DISTANCE / max_exact)
                         * (NUM_BUCKETS - max_exact)).astype(jnp.int32)
    large = jnp.minimum(large, NUM_BUCKETS - 1)
    return jnp.where(n < max_exact, n, large)


def _bias_tiles(rel_bias):
    tab = (rel_bias.astype(F32).T - rel_bias.astype(F32)[NUM_BUCKETS - 1][:, None]) * LOG2E

    def lookup(bucket):
        out = jnp.zeros((tab.shape[0],) + bucket.shape, F32)
        for b in range(NUM_BUCKETS):
            out = jnp.where(bucket[None] == b, tab[:, b][:, None, None], out)
        return out

    i = np.arange(Q_BLOCK)[:, None]
    j = np.arange(2 * CHUNK)[None, :]
    dist_near = jnp.asarray(np.where(j < CHUNK, CHUNK + i - j, i - (j - CHUNK)), jnp.int32)
    near = jnp.where(dist_near >= 0, lookup(_t5_bucket(dist_near)), NEG_INF)
    lane = np.arange(LANES)[None, :]
    dist_cmp = jnp.asarray(i - CMP_STRIDE * (lane - LANES // 2) - (CMP_LEN - 1), jnp.int32)
    cmp_ = jnp.where(dist_cmp >= 0, lookup(_t5_bucket(dist_cmp)), 0.0)
    return near, cmp_


def _compress_kernel(x_ref, pos_ref, w1_ref, w2_ref, o_ref, *, half):
    x = x_ref[...].astype(F32)
    xa = (x + pos_ref[0:1, :]).astype(BF16)
    xb = (x + pos_ref[1:2, :]).astype(BF16)
    a = jnp.dot(xa, w1_ref[0:half, :], preferred_element_type=F32)
    b = jnp.dot(xb, w1_ref[half:2 * half, :], preferred_element_type=F32)
    nc = a.shape[0]
    pre = a + pltpu.roll(b, nc - 1, 0)
    hmid = pre * jax.nn.sigmoid(pre)
    o_ref[...] = jnp.dot(hmid.astype(BF16), w2_ref[...], preferred_element_type=F32).astype(o_ref.dtype)


def _compress(kv_rows, pos, w1, w2):
    two, g, nc, wd = kv_rows.shape
    d = w2.shape[-1]
    return pl.pallas_call(
        functools.partial(_compress_kernel, half=wd),
        grid=(two, g),
        in_specs=[pl.BlockSpec((None, None, nc, wd), lambda a, b: (a, b, 0, 0)),
                  pl.BlockSpec((None, 2, wd), lambda a, b: (a, 0, 0)),
                  pl.BlockSpec((None, 2 * wd, d), lambda a, b: (a, 0, 0)),
                  pl.BlockSpec((None, d, d), lambda a, b: (a, 0, 0))],
        out_specs=pl.BlockSpec((None, None, nc, d), lambda a, b: (a, b, 0, 0)),
        out_shape=jax.ShapeDtypeStruct((two, g, nc, d), BF16),
        name="nsa_compress",
        compiler_params=_cparams(("parallel", "parallel")),
    )(kv_rows, pos, w1, w2)


def _nsa_cmp_kernel(q_ref, kc_ref, vc_ref, ov_ref, pat_ref, o_ref, imp_ref, z_ref):
    qi = pl.program_id(1)
    q0 = qi * Q_BLOCK
    ncp = kc_ref.shape[0]
    nsel = ov_ref.shape[1]
    rows = lax.broadcasted_iota(jnp.int32, (Q_BLOCK, ncp), 0)
    cols = lax.broadcasted_iota(jnp.int32, (Q_BLOCK, ncp), 1)
    add_mask = jnp.where(cols * CMP_STRIDE + (CMP_LEN - 1) <= q0 + rows, 0.0, NEG_INF)
    shift = (qi * (Q_BLOCK // CMP_STRIDE) + (ncp - LANES // 2)) % ncp
    kc = kc_ref[...]
    rhs = jnp.concatenate([vc_ref[...], ov_ref[...], jnp.ones((ncp, LANES), BF16)], axis=1)
    imp = jnp.zeros((Q_BLOCK, nsel), F32)
    for r in range(NSA_GROUP_HEADS):
        pat = pat_ref[r]
        if ncp > LANES:
            pat = jnp.concatenate([pat, jnp.zeros((Q_BLOCK, ncp - LANES), F32)], axis=1)
        bias = pltpu.roll(pat, shift, 1)
        z_ref[r] = _dot_nt(q_ref[:, r * HEAD_DIM:(r + 1) * HEAD_DIM], kc) + bias + add_mask
    for r in range(NSA_GROUP_HEADS):
        z = z_ref[r]
        m = jnp.maximum(jnp.max(z, axis=1, keepdims=True), M_INIT)
        e = jnp.exp2(z - m).astype(BF16)
        acc = jnp.dot(e, rhs, preferred_element_type=F32)
        inv = 1.0 / jnp.maximum(acc[:, HEAD_DIM + nsel:], 1e-30)
        o_ref[:, r * HEAD_DIM:(r + 1) * HEAD_DIM] = (acc[:, :HEAD_DIM] * inv).astype(o_ref.dtype)
        imp = imp + acc[:, HEAD_DIM:HEAD_DIM + nsel] * inv[:, :nsel]
    imp_ref[...] = imp


def _nsa_select_kernel(imp_ref, msel_ref, *, n_top):
    qi = pl.program_id(0)
    ng, _, nsel = imp_ref.shape
    t = qi * Q_BLOCK + lax.broadcasted_iota(jnp.int32, (Q_BLOCK, nsel), 0)
    j = lax.broadcasted_iota(jnp.int32, (Q_BLOCK, nsel), 1)
    blk_t = t // SEL_BLOCK
    forced = (j == 0) | (j == blk_t) | (j == blk_t - 1)
    visible = j * SEL_BLOCK <= t
    jf = j.astype(F32)
    imps = tuple(jnp.where(visible, jnp.where(forced, FORCE_SCORE, imp_ref[g]), -1.0) for g in range(ng))
    sels = tuple(jnp.zeros((Q_BLOCK, nsel), F32) for _ in range(ng))

    def pick(_, carry):
        out_i, out_s = [], []
        for imp_c, sel_c in zip(*carry):
            best = jnp.max(imp_c, axis=1, keepdims=True)
            first = jnp.min(jnp.where(imp_c == best, jf, float(nsel)), axis=1, keepdims=True)
            hit = jf == first
            out_i.append(jnp.where(hit, NEG_INF, imp_c))
            out_s.append(jnp.where(hit, 1.0, sel_c))
        return tuple(out_i), tuple(out_s)

    _, sels = lax.fori_loop(0, n_top, pick, (imps, sels))
    for g in range(ng):
        msel_ref[g] = jnp.where(sels[g] > 0.5, 0.0, -MASK_BIG).astype(msel_ref.dtype)


def _nsa_select(imp):
    g, s, nsel = imp.shape
    blk = pl.BlockSpec((g, Q_BLOCK, nsel), lambda i: (0, i, 0))
    return pl.pallas_call(
        functools.partial(_nsa_select_kernel, n_top=min(SEL_COUNT, nsel)),
        grid=(s // Q_BLOCK,),
        in_specs=[blk],
        out_specs=blk,
        out_shape=jax.ShapeDtypeStruct((g, s, nsel), BF16),
        name="nsa_select",
        compiler_params=_cparams(("parallel",)),
    )(imp)


def _nsa_compressed(q, kv_cmp, overlap, pat_cmp):
    s = q.shape[0]
    g, ncp, d = kv_cmp.shape[1:]
    nsel = overlap.shape[1]
    gw = NSA_GROUP_HEADS * HEAD_DIM
    return pl.pallas_call(
        _nsa_cmp_kernel,
        grid=(g, s // Q_BLOCK),
        in_specs=[pl.BlockSpec((Q_BLOCK, gw), lambda a, b: (b, a)),
                  pl.BlockSpec((None, None, ncp, d), lambda a, b: (0, a, 0, 0)),
                  pl.BlockSpec((None, None, ncp, d), lambda a, b: (1, a, 0, 0)),
                  pl.BlockSpec((ncp, nsel), lambda a, b: (0, 0)),
                  pl.BlockSpec((NSA_GROUP_HEADS, Q_BLOCK, LANES), lambda a, b: (a, 0, 0))],
        out_specs=[pl.BlockSpec((Q_BLOCK, gw), lambda a, b: (b, a)),
                   pl.BlockSpec((None, Q_BLOCK, nsel), lambda a, b: (a, b, 0))],
        out_shape=[jax.ShapeDtypeStruct((s, N_HEADS * HEAD_DIM), BF16),
                   jax.ShapeDtypeStruct((g, s, nsel), F32)],
        scratch_shapes=[pltpu.VMEM((NSA_GROUP_HEADS, Q_BLOCK, ncp), F32)],
        name="nsa_cmp_attn",
        compiler_params=_cparams(("parallel", "parallel")),
    )(q, kv_cmp, kv_cmp, overlap, pat_cmp)


def _masked_attn_kernel(*refs, nh, kv_shared, mode):
    if mode == "sel":
        q_ref, k_ref, v_ref, bias_ref, msel_ref, et_ref, o_ref, m_ref, acc_ref, z0_ref, z1_ref = refs
    else:
        q_ref, k_ref, v_ref, bias_ref, mask_ref, o_ref, m_ref, acc_ref, z0_ref, z1_ref = refs
    qi = pl.program_id(1)
    q0 = qi * Q_BLOCK
    far_end = q0 - CHUNK
    _softmax_init(m_ref, acc_ref)
    tile_chunks = KEY_TILE // CHUNK

    def head_q(h):
        return q_ref[:, h * HEAD_DIM:(h + 1) * HEAD_DIM]

    def head_k(rows, h):
        return k_ref[rows, :] if kv_shared else k_ref[rows, h * HEAD_DIM:(h + 1) * HEAD_DIM]

    def head_v(rows, h):
        return v_ref[rows, :] if kv_shared else v_ref[rows, h * HEAD_DIM:(h + 1) * HEAD_DIM]

    def mask_chunks(c0, n):
        if mode == "sel":
            return _dot_nt(msel_ref[...], et_ref[pl.ds(pl.multiple_of(c0 * CHUNK, CHUNK), n * CHUNK), :])
        parts = [mask_ref[c0 + c].astype(F32) for c in range(n)]
        return parts[0] if n == 1 else jnp.concatenate(parts, axis=1)

    n_far = lax.shift_right_logical(jnp.maximum(far_end, 0) + KEY_TILE - 1, KEY_TILE.bit_length() - 1)

    def tile_rows(kt):
        return pl.ds(pl.multiple_of(jnp.minimum(kt, n_far - 1) * KEY_TILE, KEY_TILE), KEY_TILE)

    def logits(kt, zbuf):
        cols = kt * KEY_TILE + lax.broadcasted_iota(jnp.int32, (Q_BLOCK, KEY_TILE), 1)
        chunk0 = jnp.minimum(kt, n_far - 1) * tile_chunks
        add = jnp.where(cols < far_end, mask_chunks(chunk0, tile_chunks), NEG_INF)
        rows = tile_rows(kt)
        for h in range(nh):
            zbuf[h] = _dot_nt(head_q(h), head_k(rows, h)) + add

    def consume(kt, zbuf):
        rows = tile_rows(kt)
        for h in range(nh):
            _softmax_update(zbuf[h], _with_ones(head_v(rows, h)), m_ref, acc_ref, h)

    @pl.when(n_far > 0)
    def _():
        logits(0, z0_ref)

    def far_pair(pp, carry):
        kt = 2 * pp
        logits(kt + 1, z1_ref)
        consume(kt, z0_ref)
        logits(kt + 2, z0_ref)
        consume(kt + 1, z1_ref)
        return carry

    lax.fori_loop(0, lax.shift_right_logical(n_far, 1), far_pair, 0)

    @pl.when((n_far & 1) == 1)
    def _():
        consume(n_far - 1, z0_ref)

    prev0 = pl.multiple_of(jnp.maximum(far_end, 0), CHUNK)
    diag0 = pl.multiple_of(q0, CHUNK)
    no_prev = jnp.where(qi > 0, 0.0, NEG_INF)
    add_prev = mask_chunks(jnp.maximum(qi - 1, 0), 1) + no_prev
    add_diag = mask_chunks(qi, 1)
    add_near = jnp.concatenate([add_prev, add_diag], axis=1)

    def near_kv(h):
        k = jnp.concatenate([head_k(pl.ds(prev0, CHUNK), h), head_k(pl.ds(diag0, CHUNK), h)], axis=0)
        v = jnp.concatenate([head_v(pl.ds(prev0, CHUNK), h), head_v(pl.ds(diag0, CHUNK), h)], axis=0)
        return k, _with_ones(v)

    near = 2 * CHUNK
    for h in range(nh):
        z1_ref[h, :, :near] = _dot_nt(head_q(h), near_kv(h)[0]) + (add_near + bias_ref[h])
    for h in range(nh):
        _softmax_update(z1_ref[h, :, :near], near_kv(h)[1], m_ref, acc_ref, h)
    for h in range(nh):
        o_ref[:, h * HEAD_DIM:(h + 1) * HEAD_DIM] = _normalized(acc_ref[h]).astype(o_ref.dtype)


def _masked_attention(q, k, v, bias_near, *, mode, nh, k_col0=0, v_col0=0, msel=None, expand_t=None,
                      mask=None):
    s = q.shape[0]
    n_groups = N_HEADS // nh
    qw = nh * HEAD_DIM
    kv_shared = mode == "sel"
    q_spec = pl.BlockSpec((Q_BLOCK, qw), lambda a, b: (b, a))
    bias_spec = pl.BlockSpec((nh, Q_BLOCK, 2 * CHUNK), lambda a, b: (a, 0, 0))
    if mode == "sel":
        nsel = msel.shape[-1]
        in_specs = [q_spec,
                    pl.BlockSpec((s, HEAD_DIM), lambda a, b: (0, k_col0 + a)),
                    pl.BlockSpec((s, HEAD_DIM), lambda a, b: (0, v_col0 + a)),
                    bias_spec,
                    pl.BlockSpec((None, Q_BLOCK, nsel), lambda a, b: (a, b, 0)),
                    pl.BlockSpec((s, nsel), lambda a, b: (0, 0))]
        args = [q, k, v, bias_near, msel, expand_t]
    else:
        nch = s // CHUNK
        once = pl.Buffered(1)
        in_specs = [q_spec,
                    pl.BlockSpec((s, qw), lambda a, b: (0, a), pipeline_mode=once),
                    pl.BlockSpec((s, qw), lambda a, b: (0, a), pipeline_mode=once),
                    bias_spec,
                    pl.BlockSpec((None, nch, Q_BLOCK, CHUNK), lambda a, b: (b, 0, 0, 0))]
        args = [q, k, v, bias_near, mask]
    return pl.pallas_call(
        functools.partial(_masked_attn_kernel, nh=nh, kv_shared=kv_shared, mode=mode),
        grid=(n_groups, s // Q_BLOCK),
        in_specs=in_specs,
        out_specs=pl.BlockSpec((Q_BLOCK, qw), lambda a, b: (b, a)),
        out_shape=jax.ShapeDtypeStruct((s, N_HEADS * HEAD_DIM), BF16),
        scratch_shapes=[pltpu.VMEM((nh, Q_BLOCK, LANES), F32),
                        pltpu.VMEM((nh, Q_BLOCK, 2 * HEAD_DIM), F32),
                        pltpu.VMEM((nh, Q_BLOCK, KEY_TILE), F32),
                        pltpu.VMEM((nh, Q_BLOCK, KEY_TILE), F32)],
        name="masked_attn_" + mode,
        compiler_params=_cparams(("parallel", "arbitrary")),
    )(*args)


def _nsa_win_kernel(q_ref, k_ref, v_ref, bias_ref, gl_ref, oc_ref, os_ref, o_ref, z_ref, *, n_far):
    g = pl.program_id(0)
    qi = pl.program_id(1)
    q0 = qi * Q_BLOCK
    rows = lax.broadcasted_iota(jnp.int32, (Q_BLOCK, CHUNK), 0)
    cols = lax.broadcasted_iota(jnp.int32, (Q_BLOCK, CHUNK), 1)
    adds, ks, vs = [], [], []
    for c in range(n_far + 2):
        back = n_far + 1 - c
        start = pl.multiple_of(jnp.maximum(q0 - back * CHUNK, 0), CHUNK)
        valid = qi >= back
        if c == 0:
            ok = valid & (cols > rows)
        else:
            ok = jnp.broadcast_to(valid, (Q_BLOCK, CHUNK))
        adds.append(jnp.where(ok, 0.0, NEG_INF))
        ks.append(k_ref[pl.ds(start, CHUNK), :])
        vs.append(v_ref[pl.ds(start, CHUNK), :])
    k_all = jnp.concatenate(ks, axis=0)
    v_all = _with_ones(jnp.concatenate(vs, axis=0))
    add_far = jnp.concatenate(adds[:n_far], axis=1)
    add_near = jnp.concatenate(adds[n_far:], axis=1)
    gates = jax.nn.sigmoid(gl_ref[...])
    lane = lax.broadcasted_iota(jnp.int32, (Q_BLOCK, LANES), 1)
    for h in range(NSA_GROUP_HEADS):
        add = jnp.concatenate([add_far, add_near + bias_ref[h]], axis=1)
        z_ref[h] = _dot_nt(q_ref[:, h * HEAD_DIM:(h + 1) * HEAD_DIM], k_all) + add
    for h in range(NSA_GROUP_HEADS):
        sl = slice(h * HEAD_DIM, (h + 1) * HEAD_DIM)
        head = g * NSA_GROUP_HEADS + h
        gate = [jnp.sum(jnp.where(lane == br * N_HEADS + head, gates, 0.0), axis=1, keepdims=True)
                for br in range(3)]
        o_w = _attend_once(z_ref[h], v_all)
        o = (gate[0] * oc_ref[:, sl].astype(F32) + gate[1] * os_ref[:, sl].astype(F32)
             + gate[2] * o_w)
        o_ref[:, sl] = o.astype(o_ref.dtype)


def _nsa_window_combine(q, kv, k_col0, v_col0, bias_near, gate_logits, o_c, o_s):
    s = q.shape[0]
    g = NSA_KV_GROUPS
    gw = NSA_GROUP_HEADS * HEAD_DIM
    n_far = WINDOW // CHUNK - 1
    blk = pl.BlockSpec((Q_BLOCK, gw), lambda a, b: (b, a))
    return pl.pallas_call(
        functools.partial(_nsa_win_kernel, n_far=n_far),
        grid=(g, s // Q_BLOCK),
        in_specs=[blk,
                  pl.BlockSpec((s, HEAD_DIM), lambda a, b: (0, k_col0 + a)),
                  pl.BlockSpec((s, HEAD_DIM), lambda a, b: (0, v_col0 + a)),
                  pl.BlockSpec((NSA_GROUP_HEADS, Q_BLOCK, 2 * CHUNK), lambda a, b: (a, 0, 0)),
                  pl.BlockSpec((Q_BLOCK, LANES), lambda a, b: (b, 0)),
                  blk, blk],
        out_specs=blk,
        out_shape=jax.ShapeDtypeStruct((s, N_HEADS * HEAD_DIM), BF16),
        scratch_shapes=[pltpu.VMEM((NSA_GROUP_HEADS, Q_BLOCK, WINDOW + Q_BLOCK), F32)],
        name="nsa_window",
        compiler_params=_cparams(("parallel", "parallel")),
    )(q, kv, kv, bias_near, gate_logits, o_c, o_s)


def _dsa_index_kernel(qx_ref, w_ref, kx_ref, mask_ref, key_ref, wb_ref, jp_ref, *, top_k, w_scale):
    qi = pl.program_id(0)
    q0 = qi * Q_BLOCK
    n_ch = qi + 1
    nch_total = key_ref.shape[0]
    s_total = nch_total * CHUNK
    kf = float(top_k)
    w = w_ref[...] * w_scale
    lane = lax.broadcasted_iota(jnp.int32, (Q_BLOCK, LANES), 1)
    for h in range(IDX_HEADS):
        col = jnp.sum(jnp.where(lane == h, w, 0.0), axis=1, keepdims=True)
        wb_ref[h] = jnp.broadcast_to(col, (Q_BLOCK, LANES))
    rows = lax.broadcasted_iota(jnp.int32, (Q_BLOCK, CHUNK), 0)
    cols = lax.broadcasted_iota(jnp.int32, (Q_BLOCK, CHUNK), 1)
    rows2 = lax.broadcasted_iota(jnp.int32, (Q_BLOCK, 2 * CHUNK), 0)
    cols2 = lax.broadcasted_iota(jnp.int32, (Q_BLOCK, 2 * CHUNK), 1)

    def score_pair(pp, carry):
        start = pl.multiple_of(pp * (2 * CHUNK), 2 * CHUNK)
        kx = kx_ref[pl.ds(start, 2 * CHUNK), :]
        acc = jnp.zeros((Q_BLOCK, 2 * CHUNK), F32)
        for h in range(IDX_HEADS):
            sc = _dot_nt(qx_ref[:, h * IDX_DIM:(h + 1) * IDX_DIM], kx)
            wb = wb_ref[h]
            acc = acc + jnp.maximum(sc, 0.0) * jnp.concatenate([wb, wb], axis=1)
        acc = jnp.where(start + cols2 <= q0 + rows2, acc, NEG_INF)
        bits = lax.bitcast_convert_type(acc, jnp.int32)
        keys = jnp.where(bits >= 0, bits, bits ^ jnp.int32(0x7FFFFFFF))
        key_ref[2 * pp] = keys[:, :CHUNK]
        key_ref[2 * pp + 1] = keys[:, CHUNK:]
        return carry

    n_pair = lax.shift_right_logical(n_ch + 1, 1)
    lax.fori_loop(0, n_pair, score_pair, 0)

    def count(pred):
        def body(pp, cnt):
            for c in range(2):
                cnt = cnt + jnp.where(pred(2 * pp + c, key_ref[2 * pp + c]), 1.0, 0.0)
            return cnt
        cnt = lax.fori_loop(0, n_pair, body, jnp.zeros((Q_BLOCK, CHUNK), F32))
        return jnp.sum(cnt, axis=1, keepdims=True)

    def unsettled(st):
        b, _, cnt_thr = st
        return (b < 32) & (jnp.max(jnp.abs(cnt_thr - kf)) > 0.5)

    def bisect(st):
        b, thr, cnt_thr = st
        cand = thr + lax.shift_left(jnp.int32(1), jnp.int32(31) - b)
        cand_b = jnp.broadcast_to(cand, (Q_BLOCK, CHUNK))
        cnt = count(lambda ch, key: key >= cand_b)
        ge = cnt >= kf
        return b + 1, jnp.where(ge, cand, thr), jnp.where(ge, cnt, cnt_thr)

    total = (n_pair * (2 * CHUNK)).astype(F32)
    _, thr, cnt_thr = lax.while_loop(
        unsettled, bisect,
        (jnp.int32(0), jnp.full((Q_BLOCK, 1), INT_MIN, jnp.int32), jnp.full((Q_BLOCK, 1), total, F32)))
    thr_b = jnp.broadcast_to(thr, (Q_BLOCK, CHUNK))

    tie = cnt_thr > kf
    jp_ref[...] = jnp.full((Q_BLOCK, CHUNK), s_total, jnp.int32)

    @pl.when(jnp.max(jnp.where(tie, 1.0, 0.0)) > 0.5)
    def _():
        need = kf - count(lambda ch, key: key > thr_b)

        def refine(i, jp):
            cand = jp + lax.shift_left(jnp.int32(1), jnp.int32(s_total.bit_length() - 2) - i)
            cand_b = jnp.broadcast_to(cand, (Q_BLOCK, CHUNK))
            below = count(lambda ch, key: (key == thr_b) & (ch * CHUNK + cols < cand_b))
            return jnp.where(below < need, cand, jp)

        jp = lax.fori_loop(0, s_total.bit_length() - 1, refine, jnp.zeros((Q_BLOCK, 1), jnp.int32))
        jp_ref[...] = jnp.broadcast_to(jnp.where(tie, jp, s_total), (Q_BLOCK, CHUNK))

    jp_b = jp_ref[...]

    def write_chunk(ch, carry):
        key = key_ref[ch]
        pos = ch * CHUNK + cols
        keep = ((key > thr_b) | ((key == thr_b) & (pos <= jp_b))) & (pos <= q0 + rows)
        mask_ref[ch] = jnp.where(keep, 0.0, NEG_INF).astype(mask_ref.dtype)
        return carry

    lax.fori_loop(0, n_ch, write_chunk, 0)

    def blank_chunk(ch, carry):
        mask_ref[ch] = jnp.full((Q_BLOCK, CHUNK), NEG_INF, mask_ref.dtype)
        return carry

    lax.fori_loop(n_ch, nch_total, blank_chunk, 0)


def _dsa_index_mask(q_idx, w_idx_src, w_col_block, k_idx, top_k):
    s = q_idx.shape[0]
    nq, nch = s // Q_BLOCK, s // CHUNK
    w_scale = IDX_HEADS ** -0.5 * IDX_DIM ** -0.5
    return pl.pallas_call(
        functools.partial(_dsa_index_kernel, top_k=top_k, w_scale=w_scale),
        grid=(nq,),
        in_specs=[pl.BlockSpec((Q_BLOCK, IDX_HEADS * IDX_DIM), lambda i: (i, 0)),
                  pl.BlockSpec((Q_BLOCK, LANES), lambda i: (i, w_col_block)),
                  pl.BlockSpec((s, IDX_DIM), lambda i: (0, 0))],
        out_specs=pl.BlockSpec((None, nch, Q_BLOCK, CHUNK), lambda i: (i, 0, 0, 0)),
        out_shape=jax.ShapeDtypeStruct((nq, nch, Q_BLOCK, CHUNK), BF16),
        scratch_shapes=[pltpu.VMEM((nch, Q_BLOCK, CHUNK), jnp.int32),
                        pltpu.VMEM((IDX_HEADS, Q_BLOCK, LANES), F32),
                        pltpu.VMEM((Q_BLOCK, CHUNK), jnp.int32)],
        name="dsa_index",
        compiler_params=_cparams(("parallel",)),
    )(q_idx, w_idx_src, k_idx)


def _nsa_constants(s):
    ncp = s // CMP_STRIDE
    nsel = s // SEL_BLOCK
    c0 = np.arange(ncp) * CMP_STRIDE
    sel0 = np.arange(nsel) * SEL_BLOCK
    overlap = ((c0[:, None] < sel0[None, :] + SEL_BLOCK)
               & (c0[:, None] + CMP_LEN - 1 >= sel0[None, :])).astype(np.float32)
    expand_t = (np.arange(s)[:, None] // SEL_BLOCK == np.arange(nsel)[None, :]).astype(np.float32)
    return jnp.asarray(overlap, BF16), jnp.asarray(expand_t, BF16)


def _nsa_mixer(hn, h_res, bias_near, pat_cmp, w_in, pos_k, pos_v, w1_k, w2_k, w1_v, w2_v, w_out):
    s = hn.shape[0]
    g, d = NSA_KV_GROUPS, HEAD_DIM
    hd = N_HEADS * d
    q_scale = d ** -0.5 * LOG2E
    w_q = (w_in[:, :hd] * q_scale).astype(BF16)
    w_kv = w_in[:, hd:hd + 6 * g * d].astype(BF16)
    w_g = jnp.pad(w_in[:, hd + 6 * g * d:], ((0, 0), (0, LANES - 3 * N_HEADS))).astype(BF16)
    q = _matmul(hn, w_q, BF16)
    kv = _matmul(hn, w_kv, BF16)
    gate_logits = _matmul(hn, w_g, F32)
    kv6 = kv.reshape(s, 6, g, d)
    rows16 = kv6[:, :2].transpose(1, 2, 0, 3).reshape(2, g, s // CMP_STRIDE, CMP_STRIDE * d)
    pos = jnp.stack([pos_k.reshape(2, CMP_STRIDE * d), pos_v.reshape(2, CMP_STRIDE * d)]).astype(F32)
    kv_cmp = _compress(rows16, pos,
                       jnp.stack([w1_k, w1_v]).astype(BF16), jnp.stack([w2_k, w2_v]).astype(BF16))
    overlap, expand_t = _nsa_constants(s)
    o_c, imp = _nsa_compressed(q, kv_cmp, overlap, pat_cmp)
    msel = _nsa_select(imp)
    o_s = _masked_attention(q, kv, kv, bias_near, mode="sel", nh=NSA_GROUP_HEADS, k_col0=2 * g, v_col0=3 * g,
                            msel=msel, expand_t=expand_t)
    o = _nsa_window_combine(q, kv, 4 * g, 5 * g, bias_near, gate_logits, o_c, o_s)
    return _matmul(o, w_out.astype(BF16), F32, res=h_res)


def _dsa_mixer(hn, h_res, bias_near, w_in, g_q, g_kv, w_uq, w_uk, w_uv, w_qidx, ln_g, ln_b, w_out):
    s = hn.shape[0]
    d = HEAD_DIM
    n_in = w_in.shape[1]
    w_in_p = jnp.pad(w_in, ((0, 0), (0, _round_up(n_in, LANES) - n_in))).astype(BF16)
    proj = _matmul(hn, w_in_p, F32)
    c_q = _rownorm(proj, g_q, BF16, width=Q_RANK, col_block=0)
    c_kv = _rownorm(proj, g_kv, BF16, width=KV_RANK, col_block=Q_RANK // KV_RANK)
    k_idx = _rownorm(proj, ln_g, BF16, width=IDX_DIM, col_block=(Q_RANK + KV_RANK) // IDX_DIM, bias=ln_b)
    q_scale = d ** -0.5 * LOG2E
    q = _matmul(c_q, (w_uq * q_scale).astype(BF16), BF16)
    q_idx = _matmul(c_q, w_qidx.astype(BF16), BF16)
    w_k = w_uk.transpose(2, 0, 1).reshape(KV_RANK, N_HEADS * d).astype(BF16)
    w_v = w_uv.transpose(1, 0, 2).reshape(KV_RANK, N_HEADS * d).astype(BF16)
    k = _matmul(c_kv, w_k, BF16)
    v = _matmul(c_kv, w_v, BF16)
    top_k = min(IDX_TOPK, s // 4)
    mask = _dsa_index_mask(q_idx, proj, (Q_RANK + KV_RANK + IDX_DIM) // LANES, k_idx, top_k)
    o = _masked_attention(q, k, v, bias_near, mode="dsa", nh=8, mask=mask)
    return _matmul(o, w_out.astype(BF16), F32, res=h_res)


def _ffn(hn, h_res, w_gate_all, w_up_all, w_down, layer):
    f = w_down.shape[0]
    fp = _round_up(f, 1024) if f > 1024 else _round_up(f, LANES)
    wd = jnp.pad(w_down.astype(BF16), ((0, fp - f), (0, 0)))
    return _matmul(_swiglu_up(hn, w_gate_all, w_up_all, layer, fp), wd, F32, res=h_res)


def kernel(x, rel_bias, norm_mix, norm_ffn, norm_final, ffn_w_gate, ffn_w_up, ffn_w_down, nsa_w_in, nsa_cmp_pos_k, nsa_cmp_pos_v, nsa_cmp_w1_k, nsa_cmp_w2_k, nsa_cmp_w1_v, nsa_cmp_w2_v, nsa_w_out, dsa_w_in, dsa_norm_q, dsa_norm_kv, dsa_w_uq, dsa_w_uk, dsa_w_uv, dsa_w_qidx, dsa_idx_ln_g, dsa_idx_ln_b, dsa_w_out):
    b, s, dm = x.shape
    assert s % KEY_TILE == 0 and s // CMP_STRIDE >= LANES
    depth = norm_mix.shape[0]
    bias_near, pat_cmp = _bias_tiles(rel_bias)
    outs = []
    for bi in range(b):
        h = x[bi]
        for i in range(depth):
            hn = _rownorm(h, norm_mix[i], BF16)
            a = i // 2
            if i % 2 == 0:
                h = _nsa_mixer(hn, h, bias_near, pat_cmp, nsa_w_in[a], nsa_cmp_pos_k[a], nsa_cmp_pos_v[a],
                               nsa_cmp_w1_k[a], nsa_cmp_w2_k[a], nsa_cmp_w1_v[a], nsa_cmp_w2_v[a],
                               nsa_w_out[a])
            else:
                h = _dsa_mixer(hn, h, bias_near, dsa_w_in[a], dsa_norm_q[a], dsa_norm_kv[a], dsa_w_uq[a],
                               dsa_w_uk[a], dsa_w_uv[a], dsa_w_qidx[a], dsa_idx_ln_g[a],
                               dsa_idx_ln_b[a], dsa_w_out[a])
            hn = _rownorm(h, norm_ffn[i], BF16)
            h = _ffn(hn, h, ffn_w_gate, ffn_w_up, ffn_w_down[i], i)
        outs.append(_rownorm(h, norm_final, F32))
    return jnp.stack(outs)
```

```python
import functools
import math

import numpy as np
import jax
import jax.numpy as jnp
from jax import lax
from jax.experimental import pallas as pl
from jax.experimental.pallas import tpu as pltpu

N_HEADS = 32
HEAD_DIM = 128
NUM_BUCKETS = 32
MAX_DISTANCE = 128
Q_BLOCK = 128
NSA_KV_GROUPS = 4
NSA_GROUP_HEADS = N_HEADS // NSA_KV_GROUPS
CMP_LEN = 32
CMP_STRIDE = 16
SEL_BLOCK = 64
SEL_COUNT = 16
WINDOW = 512
FORCE_SCORE = 1e4
Q_RANK = 1024
KV_RANK = 512
IDX_HEADS = 32
IDX_DIM = 128
IDX_TOPK = 256
RMS_EPS = 1e-6

LANES = 128
V7X_VMEM_LIMIT_BYTES = 56 * 1024 * 1024
KEY_TILE = 512
CHUNK = 128

F32 = jnp.float32
BF16 = jnp.bfloat16
NEG_INF = float("-inf")
M_INIT = -1e30
MASK_BIG = 2.0 ** 100
LOG2E = math.log2(math.e)
INT_MIN = -(2 ** 31)


def _cparams(sem):
    return pltpu.CompilerParams(dimension_semantics=sem,
                                vmem_limit_bytes=V7X_VMEM_LIMIT_BYTES)


def _pick(n, candidates):
    for c in candidates:
        if n % c == 0:
            return c
    return n


def _round_up(n, m):
    return (n + m - 1) // m * m


def _rmsnorm_kernel(x_ref, g_ref, o_ref):
    x = x_ref[...].astype(F32)
    ms = jnp.mean(x * x, axis=-1, keepdims=True)
    o_ref[...] = (x * lax.rsqrt(ms + RMS_EPS) * g_ref[...]).astype(o_ref.dtype)


def _layernorm_kernel(x_ref, g_ref, b_ref, o_ref):
    x = x_ref[...].astype(F32)
    mu = jnp.mean(x, axis=-1, keepdims=True)
    xc = x - mu
    var = jnp.mean(xc * xc, axis=-1, keepdims=True)
    o_ref[...] = (xc * lax.rsqrt(var + RMS_EPS) * g_ref[...] + b_ref[...]).astype(o_ref.dtype)


def _rownorm(x, gain, out_dtype, *, width=None, col_block=0, bias=None):
    m = x.shape[0]
    width = x.shape[1] if width is None else width
    tm = _pick(m, (256, 128, 64, 32, 16, 8))
    row = lambda i: (i, col_block)
    vec = lambda i: (0, 0)
    g2 = gain.reshape(1, width).astype(F32)
    if bias is None:
        kern, extra, extra_specs = _rmsnorm_kernel, (), ()
    else:
        kern, extra = _layernorm_kernel, (bias.reshape(1, width).astype(F32),)
        extra_specs = (pl.BlockSpec((1, width), vec),)
    return pl.pallas_call(
        kern,
        grid=(m // tm,),
        in_specs=[pl.BlockSpec((tm, width), row), pl.BlockSpec((1, width), vec), *extra_specs],
        out_specs=pl.BlockSpec((tm, width), lambda i: (i, 0)),
        out_shape=jax.ShapeDtypeStruct((m, width), out_dtype),
        name="rownorm",
        compiler_params=_cparams(("parallel",)),
    )(x, g2, *extra)


def _mm_kernel(*refs, nk, has_res):
    a_ref, b_ref = refs[:2]
    r_ref = refs[2] if has_res else None
    o_ref = refs[3] if has_res else refs[2]
    acc_ref = refs[-1] if nk > 1 else None
    k = pl.program_id(2)
    part = jnp.dot(a_ref[...], b_ref[...], preferred_element_type=F32)

    def finish(acc):
        if has_res:
            acc = acc + r_ref[...]
        o_ref[...] = acc.astype(o_ref.dtype)

    if nk == 1:
        finish(part)
        return

    @pl.when(k == 0)
    def _():
        acc_ref[...] = part

    @pl.when((k > 0) & (k < nk - 1))
    def _():
        acc_ref[...] += part

    @pl.when(k == nk - 1)
    def _():
        finish(acc_ref[...] + part)


MAX_FULL_K = 4096


def _matmul(a, b, out_dtype, res=None):
    m, kd = a.shape
    n = b.shape[1]
    tm = _pick(m, (1024, 512, 256, 128))
    if kd <= MAX_FULL_K:
        tk = kd
        tn_cap = min(2048, max(512, (2 * 1024 * 1024) // kd))
        tn = _pick(n, [c for c in (2048, 1024, 896, 512, 384, 256, 128) if c <= max(tn_cap, 896)])
    else:
        tk = _pick(kd, (2816, 2048, 1024, 512, 256, 128))
        tn = _pick(n, (1024, 512, 256, 128))
    nk = kd // tk
    in_specs = [pl.BlockSpec((tm, tk), lambda i, j, k: (i, k)),
                pl.BlockSpec((tk, tn), lambda i, j, k: (k, j))]
    args = [a, b]
    if res is not None:
        in_specs.append(pl.BlockSpec((tm, tn), lambda i, j, k: (i, j)))
        args.append(res)
    return pl.pallas_call(
        functools.partial(_mm_kernel, nk=nk, has_res=res is not None),
        grid=(m // tm, n // tn, nk),
        in_specs=in_specs,
        out_specs=pl.BlockSpec((tm, tn), lambda i, j, k: (i, j)),
        out_shape=jax.ShapeDtypeStruct((m, n), out_dtype),
        scratch_shapes=[pltpu.VMEM((tm, tn), F32)] if nk > 1 else [],
        name="matmul",
        compiler_params=_cparams(("parallel", "parallel", "arbitrary")),
    )(*args)


def _swiglu_kernel(x_ref, wg_ref, wu_ref, o_ref, *, n_real):
    j = pl.program_id(1)

    @pl.when(j < n_real)
    def _():
        x = x_ref[...]
        g = jnp.dot(x, wg_ref[...].astype(BF16), preferred_element_type=F32)
        u = jnp.dot(x, wu_ref[...].astype(BF16), preferred_element_type=F32)
        o_ref[...] = (g * jax.nn.sigmoid(g) * u).astype(o_ref.dtype)

    @pl.when(j >= n_real)
    def _():
        o_ref[...] = jnp.zeros_like(o_ref)


SWIGLU_TN = 256


def _swiglu_up(x, wg, wu, layer, n_out):
    m, kd = x.shape
    n = wg.shape[2]
    tn = _pick(n, (SWIGLU_TN, 128))
    assert kd <= MAX_FULL_K and n % tn == 0 and n_out % tn == 0
    n_real = n // tn
    tm = _pick(m, (2048, 1024, 512, 256, 128))
    w_spec = pl.BlockSpec((None, kd, tn), lambda i, j: (layer, 0, jnp.minimum(j, n_real - 1)))
    return pl.pallas_call(
        functools.partial(_swiglu_kernel, n_real=n_real),
        grid=(m // tm, n_out // tn),
        in_specs=[pl.BlockSpec((tm, kd), lambda i, j: (i, 0), pipeline_mode=pl.Buffered(1)),
                  w_spec, w_spec],
        out_specs=pl.BlockSpec((tm, tn), lambda i, j: (i, j)),
        out_shape=jax.ShapeDtypeStruct((m, n_out), BF16),
        name="swiglu_up",
        compiler_params=_cparams(("parallel", "arbitrary")),
    )(x, wg, wu)


def _dot_nt(a, b):
    return lax.dot_general(a, b, (((1,), (1,)), ((), ())), preferred_element_type=F32)


def _with_ones(v):
    return jnp.concatenate([v, jnp.ones_like(v)], axis=1)


def _lane_blocks(x):
    return [x[:, c * LANES:(c + 1) * LANES] for c in range(x.shape[1] // LANES)]


def _softmax_update(z, v_ones, m_ref, acc_ref, h):
    m_old = m_ref[h]
    m_new = jnp.maximum(m_old, jnp.max(z, axis=1, keepdims=True))
    alpha = jnp.exp2(m_old - m_new)
    p = jnp.concatenate([jnp.exp2(zc - m_new) for zc in _lane_blocks(z)], axis=1).astype(BF16)
    pv = jnp.dot(p, v_ones, preferred_element_type=F32)
    acc_ref[h] = jnp.concatenate([alpha, alpha], axis=1) * acc_ref[h] + pv
    m_ref[h] = m_new


def _softmax_init(m_ref, acc_ref):
    m_ref[...] = jnp.full_like(m_ref, M_INIT)
    acc_ref[...] = jnp.zeros_like(acc_ref)


def _normalized(acc):
    return acc[:, :HEAD_DIM] * (1.0 / jnp.maximum(acc[:, HEAD_DIM:], 1e-30))


def _attend_once(z, v_ones):
    m =jnp.maximum(jnp.max(z, axis=1, keepdims=True), M_INIT)
    p = jnp.exp2(z - m).astype(BF16)
    return _normalized(jnp.dot(p, v_ones, preferred_element_type=F32))


def _t5_bucket(dist):
    n = jnp.maximum(dist, 0)
    max_exact = NUM_BUCKETS // 2
    nf = jnp.maximum(n, 1).astype(F32)
    large = max_exact + (jnp.log(nf / max_exact) / math.log(MAX_DISTANCE / max_exact)
                         * (NUM_BUCKETS - max_exact)).astype(jnp.int32)
    large = jnp.minimum(large, NUM_BUCKETS - 1)
    return jnp.where(n < max_exact, n, large)


def _bias_tiles(rel_bias):
    tab = (rel_bias.astype(F32).T - rel_bias.astype(F32)[NUM_BUCKETS - 1][:, None]) * LOG2E

    def lookup(bucket):
        out = jnp.zeros((tab.shape[0],) + bucket.shape, F32)
        for b in range(NUM_BUCKETS):
            out = jnp.where(bucket[None] == b, tab[:, b][:, None, None], out)
        return out

    i = np.arange(Q_BLOCK)[:, None]
    j = np.arange(2 * CHUNK)[None, :]
    dist_near = jnp.asarray(np.where(j < CHUNK, CHUNK + i - j, i - (j - CHUNK)), jnp.int32)
    near = jnp.where(dist_near >= 0, lookup(_t5_bucket(dist_near)), NEG_INF)
    lane = np.arange(LANES)[None, :]
    dist_cmp = jnp.asarray(i - CMP_STRIDE * (lane - LANES // 2) - (CMP_LEN - 1), jnp.int32)
    cmp_ = jnp.where(dist_cmp >= 0, lookup(_t5_bucket(dist_cmp)), 0.0)
    return near, cmp_


def _compress_kernel(x_ref, pos_ref, w1_ref, w2_ref, o_ref, *, half):
    x = x_ref[...].astype(F32)
    xa = (x + pos_ref[0:1, :]).astype(BF16)
    xb = (x + pos_ref[1:2, :]).astype(BF16)
    a = jnp.dot(xa, w1_ref[0:half, :], preferred_element_type=F32)
    b = jnp.dot(xb, w1_ref[half:2 * half, :], preferred_element_type=F32)
    nc = a.shape[0]
    pre = a + pltpu.roll(b, nc - 1, 0)
    hmid = pre * jax.nn.sigmoid(pre)
    o_ref[...] = jnp.dot(hmid.astype(BF16), w2_ref[...], preferred_element_type=F32).astype(o_ref.dtype)


def _compress(kv_rows, pos, w1, w2):
    two, g, nc, wd = kv_rows.shape
    d = w2.shape[-1]
    return pl.pallas_call(
        functools.partial(_compress_kernel, half=wd),
        grid=(two, g),
        in_specs=[pl.BlockSpec((None, None, nc, wd), lambda a, b: (a, b, 0, 0)),
                  pl.BlockSpec((None, 2, wd), lambda a, b: (a, 0, 0)),
                  pl.BlockSpec((None, 2 * wd, d), lambda a, b: (a, 0, 0)),
                  pl.BlockSpec((None, d, d), lambda a, b: (a, 0, 0))],
        out_specs=pl.BlockSpec((None, None, nc, d), lambda a, b: (a, b, 0, 0)),
        out_shape=jax.ShapeDtypeStruct((two, g, nc, d), BF16),
        name="nsa_compress",
        compiler_params=_cparams(("parallel", "parallel")),
    )(kv_rows, pos, w1, w2)


def _nsa_cmp_kernel(q_ref, kc_ref, vc_ref, ov_ref, pat_ref, o_ref, imp_ref, z_ref):
    qi = pl.program_id(1)
    q0 = qi * Q_BLOCK
    ncp = kc_ref.shape[0]
    nsel = ov_ref.shape[1]
    rows = lax.broadcasted_iota(jnp.int32, (Q_BLOCK, ncp), 0)
    cols = lax.broadcasted_iota(jnp.int32, (Q_BLOCK, ncp), 1)
    add_mask = jnp.where(cols * CMP_STRIDE + (CMP_LEN - 1) <= q0 + rows, 0.0, NEG_INF)
    shift = (qi * (Q_BLOCK // CMP_STRIDE) + (ncp - LANES // 2)) % ncp
    kc = kc_ref[...]
    rhs = jnp.concatenate([vc_ref[...], ov_ref[...], jnp.ones((ncp, LANES), BF16)], axis=1)
    imp = jnp.zeros((Q_BLOCK, nsel), F32)
    for r in range(NSA_GROUP_HEADS):
        pat = pat_ref[r]
        if ncp > LANES:
            pat = jnp.concatenate([pat, jnp.zeros((Q_BLOCK, ncp - LANES), F32)], axis=1)
        bias = pltpu.roll(pat, shift, 1)
        z_ref[r] = _dot_nt(q_ref[:, r * HEAD_DIM:(r + 1) * HEAD_DIM], kc) + bias + add_mask
    for r in range(NSA_GROUP_HEADS):
        z = z_ref[r]
        m = jnp.maximum(jnp.max(z, axis=1, keepdims=True), M_INIT)
        e = jnp.exp2(z - m).astype(BF16)
        acc = jnp.dot(e, rhs, preferred_element_type=F32)
        inv = 1.0 / jnp.maximum(acc[:, HEAD_DIM + nsel:], 1e-30)
        o_ref[:, r * HEAD_DIM:(r + 1) * HEAD_DIM] = (acc[:, :HEAD_DIM] * inv).astype(o_ref.dtype)
        imp = imp + acc[:, HEAD_DIM:HEAD_DIM + nsel] * inv[:, :nsel]
    imp_ref[...] = imp


def _nsa_select_kernel(imp_ref, msel_ref, *, n_top):
    qi = pl.program_id(0)
    ng, _, nsel = imp_ref.shape
    t = qi * Q_BLOCK + lax.broadcasted_iota(jnp.int32, (Q_BLOCK, nsel), 0)
    j = lax.broadcasted_iota(jnp.int32, (Q_BLOCK, nsel), 1)
    blk_t = t // SEL_BLOCK
    forced = (j == 0) | (j == blk_t) | (j == blk_t - 1)
    visible = j * SEL_BLOCK <= t
    jf = j.astype(F32)
    imps = tuple(jnp.where(visible, jnp.where(forced, FORCE_SCORE, imp_ref[g]), -1.0) for g in range(ng))
    sels = tuple(jnp.zeros((Q_BLOCK, nsel), F32) for _ in range(ng))

    def pick(_, carry):
        out_i, out_s = [], []
        for imp_c, sel_c in zip(*carry):
            best = jnp.max(imp_c, axis=1, keepdims=True)
            first = jnp.min(jnp.where(imp_c == best, jf, float(nsel)), axis=1, keepdims=True)
            hit = jf == first
            out_i.append(jnp.where(hit, NEG_INF, imp_c))
            out_s.append(jnp.where(hit, 1.0, sel_c))
        return tuple(out_i), tuple(out_s)

    _, sels = lax.fori_loop(0, n_top, pick, (imps, sels))
    for g in range(ng):
        msel_ref[g] = jnp.where(sels[g] > 0.5, 0.0, -MASK_BIG).astype(msel_ref.dtype)


def _nsa_select(imp):
    g, s, nsel = imp.shape
    blk = pl.BlockSpec((g, Q_BLOCK, nsel), lambda i: (0, i, 0))
    return pl.pallas_call(
        functools.partial(_nsa_select_kernel, n_top=min(SEL_COUNT, nsel)),
        grid=(s // Q_BLOCK,),
        in_specs=[blk],
        out_specs=blk,
        out_shape=jax.ShapeDtypeStruct((g, s, nsel), BF16),
        name="nsa_select",
        compiler_params=_cparams(("parallel",)),
    )(imp)


def _nsa_compressed(q, kv_cmp, overlap, pat_cmp):
    s = q.shape[0]
    g, ncp, d = kv_cmp.shape[1:]
    nsel = overlap.shape[1]
    gw = NSA_GROUP_HEADS * HEAD_DIM
    return pl.pallas_call(
        _nsa_cmp_kernel,
        grid=(g, s // Q_BLOCK),
        in_specs=[pl.BlockSpec((Q_BLOCK, gw), lambda a, b: (b, a)),
                  pl.BlockSpec((None, None, ncp, d), lambda a, b: (0, a, 0, 0)),
                  pl.BlockSpec((None, None, ncp, d), lambda a, b: (1, a, 0, 0)),
                  pl.BlockSpec((ncp, nsel), lambda a, b: (0, 0)),
                  pl.BlockSpec((NSA_GROUP_HEADS, Q_BLOCK, LANES), lambda a, b: (a, 0, 0))],
        out_specs=[pl.BlockSpec((Q_BLOCK, gw), lambda a, b: (b, a)),
                   pl.BlockSpec((None, Q_BLOCK, nsel), lambda a, b: (a, b, 0))],
        out_shape=[jax.ShapeDtypeStruct((s, N_HEADS * HEAD_DIM), BF16),
                   jax.ShapeDtypeStruct((g, s, nsel), F32)],
        scratch_shapes=[pltpu.VMEM((NSA_GROUP_HEADS, Q_BLOCK, ncp), F32)],
        name="nsa_cmp_attn",
        compiler_params=_cparams(("parallel", "parallel")),
    )(q, kv_cmp, kv_cmp, overlap, pat_cmp)


def _masked_attn_kernel(*refs, nh, kv_shared, mode):
    if mode == "sel":
        q_ref, k_ref, v_ref, bias_ref, msel_ref, et_ref, o_ref, m_ref, acc_ref, z0_ref, z1_ref = refs
    else:
        q_ref, k_ref, v_ref, bias_ref, mask_ref, o_ref, m_ref, acc_ref, z0_ref, z1_ref = refs
    qi = pl.program_id(1)
    q0 = qi * Q_BLOCK
    far_end = q0 - CHUNK
    _softmax_init(m_ref, acc_ref)
    tile_chunks = KEY_TILE // CHUNK

    def head_q(h):
        return q_ref[:, h * HEAD_DIM:(h + 1) * HEAD_DIM]

    def head_k(rows, h):
        return k_ref[rows, :] if kv_shared else k_ref[rows, h * HEAD_DIM:(h + 1) * HEAD_DIM]

    def head_v(rows, h):
        return v_ref[rows, :] if kv_shared else v_ref[rows, h * HEAD_DIM:(h + 1) * HEAD_DIM]

    def mask_chunks(c0, n):
        if mode == "sel":
            return _dot_nt(msel_ref[...], et_ref[pl.ds(pl.multiple_of(c0 * CHUNK, CHUNK), n * CHUNK), :])
        parts = [mask_ref[c0 + c].astype(F32) for c in range(n)]
        return parts[0] if n == 1 else jnp.concatenate(parts, axis=1)

    n_far = lax.shift_right_logical(jnp.maximum(far_end, 0) + KEY_TILE - 1, KEY_TILE.bit_length() - 1)

    def real_tile(kt):
        return jnp.maximum(jnp.minimum(kt, n_far - 1), 0)

    def tile_rows(kt):
        return pl.ds(pl.multiple_of(real_tile(kt) * KEY_TILE, KEY_TILE), KEY_TILE)

    def logits(kt, zbuf):
        cols = kt * KEY_TILE + lax.broadcasted_iota(jnp.int32, (Q_BLOCK, KEY_TILE), 1)
        add = jnp.where(cols < far_end, mask_chunks(real_tile(kt) * tile_chunks, tile_chunks), NEG_INF)
        rows = tile_rows(kt)
        for h in range(nh):
            zbuf[h] = _dot_nt(head_q(h), head_k(rows, h)) + add

    def consume(kt, zbuf):
        rows = tile_rows(kt)
        for h in range(nh):
            _softmax_update(zbuf[h], _with_ones(head_v(rows, h)), m_ref, acc_ref, h)

    prev0 = pl.multiple_of(jnp.maximum(far_end, 0), CHUNK)
    diag0 = pl.multiple_of(q0, CHUNK)
    no_prev = jnp.where(qi > 0, 0.0, NEG_INF)
    add_prev = mask_chunks(jnp.maximum(qi - 1, 0), 1) + no_prev
    add_diag = mask_chunks(qi, 1)
    add_near = jnp.concatenate([add_prev, add_diag], axis=1)

    def near_kv(h):
        k = jnp.concatenate([head_k(pl.ds(prev0, CHUNK), h), head_k(pl.ds(diag0, CHUNK), h)], axis=0)
        v = jnp.concatenate([head_v(pl.ds(prev0, CHUNK), h), head_v(pl.ds(diag0, CHUNK), h)], axis=0)
        return k, _with_ones(v)

    near = 2 * CHUNK
    for h in range(nh):
        z1_ref[h, :, :near] = _dot_nt(head_q(h), near_kv(h)[0]) + (add_near + bias_ref[h])
    logits(0, z0_ref)
    for h in range(nh):
        _softmax_update(z1_ref[h, :, :near], near_kv(h)[1], m_ref, acc_ref, h)

    def far_pair(pp, carry):
        kt = 2 * pp
        logits(kt + 1, z1_ref)
        consume(kt, z0_ref)
        logits(kt + 2, z0_ref)
        consume(kt + 1, z1_ref)
        return carry

    lax.fori_loop(0, lax.shift_right_logical(n_far, 1), far_pair, 0)

    @pl.when((n_far & 1) == 1)
    def _():
        consume(n_far - 1, z0_ref)

    for h in range(nh):
        o_ref[:, h * HEAD_DIM:(h + 1) * HEAD_DIM] = _normalized(acc_ref[h]).astype(o_ref.dtype)


def _masked_attention(q, k, v, bias_near, *, mode, nh, k_col0=0, v_col0=0, msel=None, expand_t=None,
                      mask=None):
    s = q.shape[0]
    n_groups = N_HEADS // nh
    qw = nh * HEAD_DIM
    kv_shared = mode == "sel"
    q_spec = pl.BlockSpec((Q_BLOCK, qw), lambda a, b: (b, a))
    bias_spec = pl.BlockSpec((nh, Q_BLOCK, 2 * CHUNK), lambda a, b: (a, 0, 0))
    if mode == "sel":
        nsel = msel.shape[-1]
        in_specs = [q_spec,
                    pl.BlockSpec((s, HEAD_DIM), lambda a, b: (0, k_col0 + a)),
                    pl.BlockSpec((s, HEAD_DIM), lambda a, b: (0, v_col0 + a)),
                    bias_spec,
                    pl.BlockSpec((None, Q_BLOCK, nsel), lambda a, b: (a, b, 0)),
                    pl.BlockSpec((s, nsel), lambda a, b: (0, 0))]
        args = [q, k, v, bias_near, msel, expand_t]
    else:
        nch = s // CHUNK
        once = pl.Buffered(1)
        in_specs = [q_spec,
                    pl.BlockSpec((s, qw), lambda a, b: (0, a), pipeline_mode=once),
                    pl.BlockSpec((s, qw), lambda a, b: (0, a), pipeline_mode=once),
                    bias_spec,
                    pl.BlockSpec((None, nch, Q_BLOCK, CHUNK), lambda a, b: (b, 0, 0, 0))]
        args = [q, k, v, bias_near, mask]
    return pl.pallas_call(
        functools.partial(_masked_attn_kernel, nh=nh, kv_shared=kv_shared, mode=mode),
        grid=(n_groups, s // Q_BLOCK),
        in_specs=in_specs,
        out_specs=pl.BlockSpec((Q_BLOCK, qw), lambda a, b: (b, a)),
        out_shape=jax.ShapeDtypeStruct((s, N_HEADS * HEAD_DIM), BF16),
        scratch_shapes=[pltpu.VMEM((nh, Q_BLOCK, LANES), F32),
                        pltpu.VMEM((nh, Q_BLOCK, 2 * HEAD_DIM), F32),
                        pltpu.VMEM((nh, Q_BLOCK, KEY_TILE), F32),
                        pltpu.VMEM((nh, Q_BLOCK, KEY_TILE), F32)],
        name="masked_attn_" + mode,
        compiler_params=_cparams(("parallel", "arbitrary")),
    )(*args)


def _nsa_win_kernel(q_ref, k_ref, v_ref, bias_ref, gl_ref, oc_ref, os_ref, o_ref, z_ref, *, n_far):
    g = pl.program_id(0)
    qi = pl.program_id(1)
    q0 = qi * Q_BLOCK
    rows = lax.broadcasted_iota(jnp.int32, (Q_BLOCK, CHUNK), 0)
    cols = lax.broadcasted_iota(jnp.int32, (Q_BLOCK, CHUNK), 1)
    adds, ks, vs = [], [], []
    for c in range(n_far + 2):
        back = n_far + 1 - c
        start = pl.multiple_of(jnp.maximum(q0 - back * CHUNK, 0), CHUNK)
        valid = qi >= back
        if c == 0:
            ok = valid & (cols > rows)
        else:
            ok = jnp.broadcast_to(valid, (Q_BLOCK, CHUNK))
        adds.append(jnp.where(ok, 0.0, NEG_INF))
        ks.append(k_ref[pl.ds(start, CHUNK), :])
        vs.append(v_ref[pl.ds(start, CHUNK), :])
    k_all = jnp.concatenate(ks, axis=0)
    v_all = _with_ones(jnp.concatenate(vs, axis=0))
    add_far = jnp.concatenate(adds[:n_far], axis=1)
    add_near = jnp.concatenate(adds[n_far:], axis=1)
    gates = jax.nn.sigmoid(gl_ref[...])
    lane = lax.broadcasted_iota(jnp.int32, (Q_BLOCK, LANES), 1)
    for h in range(NSA_GROUP_HEADS):
        add = jnp.concatenate([add_far, add_near + bias_ref[h]], axis=1)
        z_ref[h] = _dot_nt(q_ref[:, h * HEAD_DIM:(h + 1) * HEAD_DIM], k_all) + add
    for h in range(NSA_GROUP_HEADS):
        sl = slice(h * HEAD_DIM, (h + 1) * HEAD_DIM)
        head = g * NSA_GROUP_HEADS + h
        gate = [jnp.sum(jnp.where(lane == br * N_HEADS + head, gates, 0.0), axis=1, keepdims=True)
                for br in range(3)]
        o_w = _attend_once(z_ref[h], v_all)
        o = (gate[0] * oc_ref[:, sl].astype(F32) + gate[1] * os_ref[:, sl].astype(F32)
             + gate[2] * o_w)
        o_ref[:, sl] = o.astype(o_ref.dtype)


def _nsa_window_combine(q, kv, k_col0, v_col0, bias_near, gate_logits, o_c, o_s):
    s = q.shape[0]
    g = NSA_KV_GROUPS
    gw = NSA_GROUP_HEADS * HEAD_DIM
    n_far = WINDOW // CHUNK - 1
    blk = pl.BlockSpec((Q_BLOCK, gw), lambda a, b: (b, a))
    return pl.pallas_call(
        functools.partial(_nsa_win_kernel, n_far=n_far),
        grid=(g, s // Q_BLOCK),
        in_specs=[blk,
                  pl.BlockSpec((s, HEAD_DIM), lambda a, b: (0, k_col0 + a)),
                  pl.BlockSpec((s, HEAD_DIM), lambda a, b: (0, v_col0 + a)),
                  pl.BlockSpec((NSA_GROUP_HEADS, Q_BLOCK, 2 * CHUNK), lambda a, b: (a, 0, 0)),
                  pl.BlockSpec((Q_BLOCK, LANES), lambda a, b: (b, 0)),
                  blk, blk],
        out_specs=blk,
        out_shape=jax.ShapeDtypeStruct((s, N_HEADS * HEAD_DIM), BF16),
        scratch_shapes=[pltpu.VMEM((NSA_GROUP_HEADS, Q_BLOCK, WINDOW + Q_BLOCK), F32)],
        name="nsa_window",
        compiler_params=_cparams(("parallel", "parallel")),
    )(q, kv, kv, bias_near, gate_logits, o_c, o_s)


def _dsa_index_kernel(qx_ref, w_ref, kx_ref, mask_ref, key_ref, wb_ref, jp_ref, *, top_k, w_scale):
    qi = pl.program_id(0)
    q0 = qi * Q_BLOCK
    n_ch = qi + 1
    nch_total = key_ref.shape[0]
    s_total = nch_total * CHUNK
    kf = float(top_k)
    w = w_ref[...] * w_scale
    lane = lax.broadcasted_iota(jnp.int32, (Q_BLOCK, LANES), 1)
    for h in range(IDX_HEADS):
        col = jnp.sum(jnp.where(lane == h, w, 0.0), axis=1, keepdims=True)
        wb_ref[h] = jnp.broadcast_to(col, (Q_BLOCK, LANES))
    rows = lax.broadcasted_iota(jnp.int32, (Q_BLOCK, CHUNK), 0)
    cols = lax.broadcasted_iota(jnp.int32, (Q_BLOCK, CHUNK), 1)
    rows2 = lax.broadcasted_iota(jnp.int32, (Q_BLOCK, 2 * CHUNK), 0)
    cols2 = lax.broadcasted_iota(jnp.int32, (Q_BLOCK, 2 * CHUNK), 1)

    def score_pair(pp, carry):
        start = pl.multiple_of(pp * (2 * CHUNK), 2 * CHUNK)
        kx = kx_ref[pl.ds(start, 2 * CHUNK), :]
        acc = jnp.zeros((Q_BLOCK, 2 * CHUNK), F32)
        for h in range(IDX_HEADS):
            sc = _dot_nt(qx_ref[:, h * IDX_DIM:(h + 1) * IDX_DIM], kx)
            wb = wb_ref[h]
            acc = acc + jnp.maximum(sc, 0.0) * jnp.concatenate([wb, wb], axis=1)
        acc = jnp.where(start + cols2 <= q0 + rows2, acc, NEG_INF)
        bits = lax.bitcast_convert_type(acc, jnp.int32)
        keys = jnp.where(bits >= 0, bits, bits ^ jnp.int32(0x7FFFFFFF))
        key_ref[2 * pp] = keys[:, :CHUNK]
        key_ref[2 * pp + 1] = keys[:, CHUNK:]
        return carry

    n_pair = lax.shift_right_logical(n_ch + 1, 1)
    lax.fori_loop(0, n_pair, score_pair, 0)

    def count(pred):
        def body(pp, cnt):
            for c in range(2):
                cnt = cnt + jnp.where(pred(2 * pp + c, key_ref[2 * pp + c]), 1.0, 0.0)
            return cnt
        cnt = lax.fori_loop(0, n_pair, body, jnp.zeros((Q_BLOCK, CHUNK), F32))
        return jnp.sum(cnt, axis=1, keepdims=True)

    def unsettled(st):
        b, _, cnt_thr = st
        return (b < 32) & (jnp.max(jnp.abs(cnt_thr - kf)) > 0.5)

    def bisect(st):
        b, thr, cnt_thr = st
        cand = thr + lax.shift_left(jnp.int32(1), jnp.int32(31) - b)
        cand_b = jnp.broadcast_to(cand, (Q_BLOCK, CHUNK))
        cnt = count(lambda ch, key: key >= cand_b)
        ge = cnt >= kf
        return b + 1, jnp.where(ge, cand, thr), jnp.where(ge, cnt, cnt_thr)

    total = (n_pair * (2 * CHUNK)).astype(F32)
    _, thr, cnt_thr = lax.while_loop(
        unsettled, bisect,
        (jnp.int32(0), jnp.full((Q_BLOCK, 1), INT_MIN, jnp.int32), jnp.full((Q_BLOCK, 1), total, F32)))
    thr_b = jnp.broadcast_to(thr, (Q_BLOCK, CHUNK))

    tie = cnt_thr > kf
    jp_ref[...] = jnp.full((Q_BLOCK, CHUNK), s_total, jnp.int32)

    @pl.when(jnp.max(jnp.where(tie, 1.0, 0.0)) > 0.5)
    def _():
        need = kf - count(lambda ch, key: key > thr_b)

        def refine(i, jp):
            cand = jp + lax.shift_left(jnp.int32(1), jnp.int32(s_total.bit_length() - 2) - i)
            cand_b = jnp.broadcast_to(cand, (Q_BLOCK, CHUNK))
            below = count(lambda ch, key: (key == thr_b) & (ch * CHUNK + cols < cand_b))
            return jnp.where(below < need, cand, jp)

        jp = lax.fori_loop(0, s_total.bit_length() - 1, refine, jnp.zeros((Q_BLOCK, 1), jnp.int32))
        jp_ref[...] = jnp.broadcast_to(jnp.where(tie, jp, s_total), (Q_BLOCK, CHUNK))

    jp_b = jp_ref[...]

    def write_chunk(ch, carry):
        key = key_ref[ch]
        pos = ch * CHUNK + cols
        keep = ((key > thr_b) | ((key == thr_b) & (pos <= jp_b))) & (pos <= q0 + rows)
        mask_ref[ch] = jnp.where(keep, 0.0, NEG_INF).astype(mask_ref.dtype)
        return carry

    lax.fori_loop(0, n_ch, write_chunk, 0)

    def blank_chunk(ch, carry):
        mask_ref[ch] = jnp.full((Q_BLOCK, CHUNK), NEG_INF, mask_ref.dtype)
        return carry

    lax.fori_loop(n_ch, nch_total, blank_chunk, 0)


def _dsa_index_mask(q_idx, w_idx_src, w_col_block, k_idx, top_k):
    s = q_idx.shape[0]
    nq, nch = s // Q_BLOCK, s // CHUNK
    w_scale = IDX_HEADS ** -0.5 * IDX_DIM ** -0.5
    return pl.pallas_call(
        functools.partial(_dsa_index_kernel, top_k=top_k, w_scale=w_scale),
        grid=(nq,),
        in_specs=[pl.BlockSpec((Q_BLOCK, IDX_HEADS * IDX_DIM), lambda i: (i, 0)),
                  pl.BlockSpec((Q_BLOCK, LANES), lambda i: (i, w_col_block)),
                  pl.BlockSpec((s, IDX_DIM), lambda i: (0, 0))],
        out_specs=pl.BlockSpec((None, nch, Q_BLOCK, CHUNK), lambda i: (i, 0, 0, 0)),
        out_shape=jax.ShapeDtypeStruct((nq, nch, Q_BLOCK, CHUNK), BF16),
        scratch_shapes=[pltpu.VMEM((nch, Q_BLOCK, CHUNK), jnp.int32),
                        pltpu.VMEM((IDX_HEADS, Q_BLOCK, LANES), F32),
                        pltpu.VMEM((Q_BLOCK, CHUNK), jnp.int32)],
        name="dsa_index",
        compiler_params=_cparams(("parallel",)),
    )(q_idx, w_idx_src, k_idx)


def _nsa_constants(s):
    ncp = s // CMP_STRIDE
    nsel = s // SEL_BLOCK
    c0 = np.arange(ncp) * CMP_STRIDE
    sel0 = np.arange(nsel) * SEL_BLOCK
    overlap = ((c0[:, None] < sel0[None, :] + SEL_BLOCK)
               & (c0[:, None] + CMP_LEN - 1 >= sel0[None, :])).astype(np.float32)
    expand_t = (np.arange(s)[:, None] // SEL_BLOCK == np.arange(nsel)[None, :]).astype(np.float32)
    return jnp.asarray(overlap, BF16), jnp.asarray(expand_t, BF16)


def _nsa_mixer(hn, h_res, bias_near, pat_cmp, w_in, pos_k, pos_v, w1_k, w2_k, w1_v, w2_v, w_out):
    s = hn.shape[0]
    g, d = NSA_KV_GROUPS, HEAD_DIM
    hd = N_HEADS * d
    q_scale = d ** -0.5 * LOG2E
    w_q = (w_in[:, :hd] * q_scale).astype(BF16)
    w_kv = w_in[:, hd:hd + 6 * g * d].astype(BF16)
    w_g = jnp.pad(w_in[:, hd + 6 * g * d:], ((0, 0), (0, LANES - 3 * N_HEADS))).astype(BF16)
    q = _matmul(hn, w_q, BF16)
    kv = _matmul(hn, w_kv, BF16)
    gate_logits = _matmul(hn, w_g, F32)
    kv6 = kv.reshape(s, 6, g, d)
    rows16 = kv6[:, :2].transpose(1, 2, 0, 3).reshape(2, g, s // CMP_STRIDE, CMP_STRIDE * d)
    pos = jnp.stack([pos_k.reshape(2, CMP_STRIDE * d), pos_v.reshape(2, CMP_STRIDE * d)]).astype(F32)
    kv_cmp = _compress(rows16, pos,
                       jnp.stack([w1_k, w1_v]).astype(BF16), jnp.stack([w2_k, w2_v]).astype(BF16))
    overlap, expand_t = _nsa_constants(s)
    o_c, imp = _nsa_compressed(q, kv_cmp, overlap, pat_cmp)
    msel = _nsa_select(imp)
    o_s = _masked_attention(q, kv, kv, bias_near, mode="sel", nh=NSA_GROUP_HEADS, k_col0=2 * g, v_col0=3 * g,
                            msel=msel, expand_t=expand_t)
    o = _nsa_window_combine(q, kv, 4 * g, 5 * g, bias_near, gate_logits, o_c, o_s)
    return _matmul(o, w_out.astype(BF16), F32, res=h_res)


def _dsa_mixer(hn, h_res, bias_near, w_in, g_q, g_kv, w_uq, w_uk, w_uv, w_qidx, ln_g, ln_b, w_out):
    s = hn.shape[0]
    d = HEAD_DIM
    n_in = w_in.shape[1]
    w_in_p = jnp.pad(w_in, ((0, 0), (0, _round_up(n_in, LANES) - n_in))).astype(BF16)
    proj = _matmul(hn, w_in_p, F32)
    c_q = _rownorm(proj, g_q, BF16, width=Q_RANK, col_block=0)
    c_kv = _rownorm(proj, g_kv, BF16, width=KV_RANK, col_block=Q_RANK // KV_RANK)
    k_idx = _rownorm(proj, ln_g, BF16, width=IDX_DIM, col_block=(Q_RANK + KV_RANK) // IDX_DIM, bias=ln_b)
    q_scale = d ** -0.5 * LOG2E
    q = _matmul(c_q, (w_uq * q_scale).astype(BF16), BF16)
    q_idx = _matmul(c_q, w_qidx.astype(BF16), BF16)
    w_k = w_uk.transpose(2, 0, 1).reshape(KV_RANK, N_HEADS * d).astype(BF16)
    w_v = w_uv.transpose(1, 0, 2).reshape(KV_RANK, N_HEADS * d).astype(BF16)
    k = _matmul(c_kv, w_k, BF16)
    v = _matmul(c_kv, w_v, BF16)
    top_k = min(IDX_TOPK, s // 4)
    mask = _dsa_index_mask(q_idx, proj, (Q_RANK + KV_RANK + IDX_DIM) // LANES, k_idx, top_k)
    o = _masked_attention(q, k, v, bias_near, mode="dsa", nh=8, mask=mask)
    return _matmul(o, w_out.astype(BF16), F32, res=h_res)


def _ffn(hn, h_res, w_gate_all, w_up_all, w_down, layer):
    f = w_down.shape[0]
    fp = _round_up(f, 1024) if f > 1024 else _round_up(f, LANES)
    wd = jnp.pad(w_down.astype(BF16), ((0, fp - f), (0, 0)))
    return _matmul(_swiglu_up(hn, w_gate_all, w_up_all, layer, fp), wd, F32, res=h_res)


def kernel(x, rel_bias, norm_mix, norm_ffn, norm_final, ffn_w_gate, ffn_w_up, ffn_w_down, nsa_w_in, nsa_cmp_pos_k, nsa_cmp_pos_v, nsa_cmp_w1_k, nsa_cmp_w2_k, nsa_cmp_w1_v, nsa_cmp_w2_v, nsa_w_out, dsa_w_in, dsa_norm_q, dsa_norm_kv, dsa_w_uq, dsa_w_uk, dsa_w_uv, dsa_w_qidx, dsa_idx_ln_g, dsa_idx_ln_b, dsa_w_out):
    b, s, dm = x.shape
    assert s % KEY_TILE == 0 and s // CMP_STRIDE >= LANES
    depth = norm_mix.shape[0]
    bias_near, pat_cmp = _bias_tiles(rel_bias)
    outs = []
    for bi in range(b):
        h = x[bi]
        for i in range(depth):
            hn = _rownorm(h, norm_mix[i], BF16)
            a = i // 2
            if i % 2 == 0:
                h = _nsa_mixer(hn, h, bias_near, pat_cmp, nsa_w_in[a], nsa_cmp_pos_k[a], nsa_cmp_pos_v[a],
                               nsa_cmp_w1_k[a], nsa_cmp_w2_k[a], nsa_cmp_w1_v[a], nsa_cmp_w2_v[a],
                               nsa_w_out[a])
            else:
                h = _dsa_mixer(hn, h, bias_near, dsa_w_in[a], dsa_norm_q[a], dsa_norm_kv[a], dsa_w_uq[a],
                               dsa_w_uk[a], dsa_w_uv[a], dsa_w_qidx[a], dsa_idx_ln_g[a],
                               dsa_idx_ln_b[a], dsa_w_out[a])
            hn = _rownorm(h, norm_ffn[i], BF16)
            h = _ffn(hn, h, ffn_w_gate, ffn_w_up, ffn_w_down[i], i)
        outs.append(_rownorm(h, norm_final, F32))
    return jnp.stack(outs)
```
